```python
import jax, jax.numpy as jnp
from jax import lax
import numpy as np

D_MODEL = 1024
BATCH = 4
SEQ = 8192
DEPTH = 2

CHUNK = 64
N_MIXERS = 2
D_FF = 4 * D_MODEL
FFN_RES = 0.5
EPS = 1e-6
CONV_W = 4
ML_HEADS = 8
ML_DV = D_MODEL // ML_HEADS
ML_DQK = ML_DV // 2
ML_COLS = 2 * ML_HEADS * ML_DQK + 2 * ML_HEADS * ML_DV + 2 * ML_HEADS
FX_DH = 64
FX_HEADS = D_MODEL // FX_DH
FX_COLS = 4 * FX_HEADS * FX_DH + FX_HEADS
Q_BLOCK = 128
N_A = (DEPTH + 1) // 2
N_B = DEPTH // 2
NEG = -1e30

kernel_name = "hybrid_mlstm_fox_macaron_adaln"


def rms_norm(x, g):
    xf = x.astype(jnp.float32)
    y = xf * lax.rsqrt(jnp.mean(xf * xf, axis=-1, keepdims=True) + EPS)
    return (y * g.astype(jnp.float32)).astype(x.dtype)


def adaln_in(x, g, shift, scale):
    return rms_norm(x, g) * (1.0 + scale[:, None, :]) + shift[:, None, :]


def swiglu(h, w_in, w_out):
    gate, up = jnp.split(h @ w_in, 2, axis=-1)
    return (jax.nn.silu(gate) * up) @ w_out


def causal_conv(x, w):
    k = w.shape[0]
    return lax.conv_general_dilated(
        x, w[:, None, :].astype(x.dtype), window_strides=(1,), padding=[(k - 1, 0)],
        dimension_numbers=("NWC", "WIO", "NWC"), feature_group_count=x.shape[-1])


def mlstm_mixer(h, w_in, conv_w, b_i, b_f, hnorm_g, w_out):
    B, S, _ = h.shape
    H, dk, dv = ML_HEADS, ML_DQK, ML_DV
    f32 = jnp.float32
    proj = h @ w_in
    qk, v, o, i_pre, f_pre = jnp.split(
        proj, [2 * H * dk, 2 * H * dk + H * dv, 2 * H * dk + 2 * H * dv, 2 * H * dk + 2 * H * dv + H], axis=-1)
    qk = jax.nn.silu(causal_conv(qk, conv_w))
    q, k = jnp.split(qk, 2, axis=-1)
    i_pre = i_pre.astype(f32) + b_i.astype(f32)
    logf = jax.nn.log_sigmoid(f_pre.astype(f32) + b_f.astype(f32))
    nc = S // CHUNK

    def chunks(t, d):
        return t.reshape(B, nc, CHUNK, H, d).transpose(1, 0, 3, 2, 4).astype(f32)

    def gchunks(t):
        return t.reshape(B, nc, CHUNK, H).transpose(1, 0, 3, 2)

    qc = chunks(q, dk) * (dk ** -0.5)
    kc = chunks(k, dk)
    vc = chunks(v, dv)
    ic, fc = gchunks(i_pre), gchunks(logf)
    causal = jnp.tril(jnp.ones((CHUNK, CHUNK), dtype=bool))

    def step(carry, inp):
        C, n, m = carry
        q_, k_, v_, i_, lf = inp
        b = jnp.cumsum(lf, axis=-1)
        g = b[..., -1]
        dmat = jnp.where(causal, b[..., :, None] - b[..., None, :] + i_[..., None, :], -jnp.inf)
        inter = b + m[..., None]
        m_t = jnp.maximum(inter, jnp.max(dmat, axis=-1))
        w_intra = jnp.exp(dmat - m_t[..., None])
        w_inter = jnp.exp(inter - m_t)
        s = jnp.einsum("bhld,bhsd->bhls", q_, k_) * w_intra
        num = jnp.einsum("bhls,bhsv->bhlv", s, v_) + w_inter[..., None] * jnp.einsum("bhld,bhdv->bhlv", q_, C)
        den = jnp.sum(s, axis=-1) + w_inter * jnp.einsum("bhld,bhd->bhl", q_, n)
        h_t = num / jnp.maximum(jnp.abs(den), jnp.exp(-m_t))[..., None]
        a = g[..., None] - b + i_
        m_new = jnp.maximum(g + m, jnp.max(a, axis=-1))
        wk = jnp.exp(a - m_new[..., None])
        decay = jnp.exp(g + m - m_new)
        C_new = decay[..., None, None] * C + jnp.einsum("bhs,bhsd,bhsv->bhdv", wk, k_, v_)
        n_new = decay[..., None] * n + jnp.einsum("bhs,bhsd->bhd", wk, k_)
        return (C_new, n_new, m_new), h_t

    init = (jnp.zeros((B, H, dk, dv), f32), jnp.zeros((B, H, dk), f32), jnp.zeros((B, H), f32))
    _, hs = lax.scan(step, init, (qc, kc, vc, ic, fc))
    hs = hs.transpose(1, 0, 3, 2, 4).reshape(B, S, H, dv).astype(h.dtype)
    hs = rms_norm(hs, hnorm_g.reshape(H, dv)).reshape(B, S, H * dv)
    return (jax.nn.sigmoid(o) * hs) @ w_out


def fox_mixer(h, w_in, b_f, qn_g, kn_g, w_out):
    B, S, _ = h.shape
    H, dh = FX_HEADS, FX_DH
    f32 = jnp.float32
    proj = h @ w_in
    q, k, v, og, f_pre = jnp.split(proj, [H * dh, 2 * H * dh, 3 * H * dh, 4 * H * dh], axis=-1)
    q = rms_norm(q.reshape(B, S, H, dh), qn_g)
    k = rms_norm(k.reshape(B, S, H, dh), kn_g)
    v = v.reshape(B, S, H, dh)
    logf = jax.nn.log_sigmoid(f_pre.astype(f32) + b_f.astype(f32))
    F = jnp.cumsum(logf, axis=1).transpose(0, 2, 1)
    kT = k.transpose(0, 2, 1, 3).astype(f32)
    vT = v.transpose(0, 2, 1, 3).astype(f32)
    nq = S // Q_BLOCK
    q_blocks = q.transpose(0, 2, 1, 3).reshape(B, H, nq, Q_BLOCK, dh).transpose(2, 0, 1, 3, 4)
    F_blocks = F.reshape(B, H, nq, Q_BLOCK).transpose(2, 0, 1, 3)
    scale = dh ** -0.5

    def q_block(args):
        qi, qb, Fq = args
        qb = qb.astype(f32) * scale
        q_pos = qi * Q_BLOCK + jnp.arange(Q_BLOCK)

        def kv_step(kj, carry):
            m, l, acc = carry
            start = kj * Q_BLOCK
            kb = lax.dynamic_slice_in_dim(kT, start, Q_BLOCK, axis=2)
            vb = lax.dynamic_slice_in_dim(vT, start, Q_BLOCK, axis=2)
            Fk = lax.dynamic_slice_in_dim(F, start, Q_BLOCK, axis=2)
            k_pos = start + jnp.arange(Q_BLOCK)
            s = jnp.einsum("bhqd,bhkd->bhqk", qb, kb) + Fq[..., :, None] - Fk[..., None, :]
            s = jnp.where(q_pos[:, None] >= k_pos[None, :], s, NEG)
            m_new = jnp.maximum(m, jnp.max(s, axis=-1))
            p = jnp.exp(s - m_new[..., None])
            corr = jnp.exp(m - m_new)
            l = corr * l + jnp.sum(p, axis=-1)
            acc = corr[..., None] * acc + jnp.einsum("bhqk,bhkd->bhqd", p, vb)
            return m_new, l, acc

        init = (jnp.full((B, H, Q_BLOCK), NEG, f32), jnp.zeros((B, H, Q_BLOCK), f32),
                jnp.zeros((B, H, Q_BLOCK, dh), f32))
        _, l, acc = lax.fori_loop(0, qi + 1, kv_step, init)
        return acc / l[..., None]

    out = lax.map(q_block, (jnp.arange(nq), q_blocks, F_blocks))
    out = out.transpose(1, 0, 3, 2, 4).reshape(B, S, H * dh).astype(h.dtype)
    return (jax.nn.sigmoid(og) * out) @ w_out


def setup_inputs(seed: int = 0) -> dict:
    key = jax.random.key(seed)
    ks = jax.random.split(key, 20)
    D, F = D_MODEL, D_FF
    nrm = jax.random.normal
    return {
        "x": nrm(ks[0], (BATCH, SEQ, D), jnp.float32),
        "c": nrm(ks[1], (BATCH, D), jnp.float32),
        "ada_w": nrm(ks[2], (DEPTH, D, 9 * D), jnp.float32) * (0.1 * D ** -0.5),
        "ada_b": nrm(ks[3], (DEPTH, 9 * D), jnp.float32) * 0.01,
        "norm_g": 1.0 + 0.02 * nrm(ks[4], (DEPTH, 3, D), jnp.float32),
        "ffn_w_in": nrm(ks[5], (DEPTH, 2, D, 2 * F), jnp.float32) * D ** -0.5,
        "ffn_w_out": nrm(ks[6], (DEPTH, 2, F, D), jnp.float32) * F ** -0.5,
        "ml_w_in": nrm(ks[7], (N_A, D, ML_COLS), jnp.float32) * D ** -0.5,
        "ml_conv_w": nrm(ks[8], (N_A, CONV_W, 2 * ML_HEADS * ML_DQK), jnp.float32) * CONV_W ** -0.5,
        "ml_b_i": nrm(ks[9], (N_A, ML_HEADS), jnp.float32) * 0.1,
        "ml_b_f": jax.random.uniform(ks[10], (N_A, ML_HEADS), jnp.float32, 3.0, 6.0),
        "ml_hnorm_g": 1.0 + 0.02 * nrm(ks[11], (N_A, ML_HEADS * ML_DV), jnp.float32),
        "ml_w_out": nrm(ks[12], (N_A, ML_HEADS * ML_DV, D), jnp.float32) * (ML_HEADS * ML_DV) ** -0.5,
        "fx_w_in": nrm(ks[13], (N_B, D, FX_COLS), jnp.float32) * D ** -0.5,
        "fx_b_f": jax.random.uniform(ks[14], (N_B, FX_HEADS), jnp.float32, 1.0, 4.0),
        "fx_qnorm_g": 1.0 + 0.02 * nrm(ks[15], (N_B, FX_DH), jnp.float32),
        "fx_knorm_g": 1.0 + 0.02 * nrm(ks[16], (N_B, FX_DH), jnp.float32),
        "fx_w_out": nrm(ks[17], (N_B, FX_HEADS * FX_DH, D), jnp.float32) * (FX_HEADS * FX_DH) ** -0.5,
        "final_g": 1.0 + 0.02 * nrm(ks[18], (D,), jnp.float32),
    }


def reference(x, c, ada_w, ada_b, norm_g, ffn_w_in, ffn_w_out, ml_w_in, ml_conv_w, ml_b_i, ml_b_f,
              ml_hnorm_g, ml_w_out, fx_w_in, fx_b_f, fx_qnorm_g, fx_knorm_g, fx_w_out, final_g):
    B = x.shape[0]
    cond = jax.nn.silu(c)
    for layer in range(DEPTH):
        mod = (cond @ ada_w[layer] + ada_b[layer]).reshape(B, 3, 3, D_MODEL)
        h = adaln_in(x, norm_g[layer, 0], mod[:, 0, 0], mod[:, 0, 1])
        x = x + FFN_RES * (1.0 + mod[:, 0, 2])[:, None, :] * swiglu(h, ffn_w_in[layer, 0], ffn_w_out[layer, 0])
        h = adaln_in(x, norm_g[layer, 1], mod[:, 1, 0], mod[:, 1, 1])
        j = layer // N_MIXERS
        if layer % N_MIXERS == 0:
            y = mlstm_mixer(h, ml_w_in[j], ml_conv_w[j], ml_b_i[j], ml_b_f[j], ml_hnorm_g[j], ml_w_out[j])
        else:
            y = fox_mixer(h, fx_w_in[j], fx_b_f[j], fx_qnorm_g[j], fx_knorm_g[j], fx_w_out[j])
        x = x + (1.0 + mod[:, 1, 2])[:, None, :] * y
        h = adaln_in(x, norm_g[layer, 2], mod[:, 2, 0], mod[:, 2, 1])
        x = x + FFN_RES * (1.0 + mod[:, 2, 2])[:, None, :] * swiglu(h, ffn_w_in[layer, 1], ffn_w_out[layer, 1])
    return rms_norm(x, final_g)
```

```python
import functools

import jax
import jax.numpy as jnp
from jax import lax
from jax.experimental import pallas as pl
from jax.experimental.pallas import tpu as pltpu

D_MODEL = 1024
D_FF = 4 * D_MODEL
EPS = 1e-6
FFN_RES = 0.5
CONV_W = 4
ML_HEADS = 8
ML_DV = 128
ML_DQK = 64
FX_HEADS = 16
FX_DH = 64
NEG = -1e30

LANES = 128
SUBLANES = 8
VMEM_LIMIT_BYTES = 56 * 1024 * 1024

FFN_TM = 1024
FFN_TF = 512
PROJ_TM = 512
PROJ_TN = 512
ML_CHUNK = 128
FX_TS = 512
FX_TQ = 512
FX_TK = 512
FX_BIAS_LANE = FX_DH

BF16 = jnp.bfloat16
F32 = jnp.float32


def _cparams(sem):
    return pltpu.CompilerParams(dimension_semantics=sem, vmem_limit_bytes=VMEM_LIMIT_BYTES)


def _dot(a, b):
    return jnp.dot(a, b, preferred_element_type=F32)


def _dot_nt(a, b):
    return lax.dot_general(a, b, (((1,), (1,)), ((), ())), preferred_element_type=F32)


def _dot_tn(a, b):
    return lax.dot_general(a, b, (((0,), (0,)), ((), ())), preferred_element_type=F32)


def _sigmoid(x):
    return 1.0 / (1.0 + jnp.exp(-x))


def _log_sigmoid(x):
    return jnp.minimum(x, 0.0) - jnp.log(1.0 + jnp.exp(-jnp.abs(x)))


def _split3(x):
    hi = x.astype(BF16)
    r1 = x - hi.astype(F32)
    mid = r1.astype(BF16)
    lo = (r1 - mid.astype(F32)).astype(BF16)
    return hi, mid, lo


def _adaln(x, g, shift, scale):
    ms = jnp.mean(x * x, axis=-1, keepdims=True)
    return (x * lax.rsqrt(ms + EPS) * g) * (1.0 + scale) + shift


def _mod_kernel(c_ref, w_ref, b_ref, o_ref):
    c = c_ref[...]
    cond = (c * _sigmoid(c)).astype(BF16)
    o_ref[0] = _dot(cond, w_ref[0].astype(BF16)) + b_ref[0]


def _modulation(c, ada_w, ada_b):
    depth, d, n = ada_w.shape
    b = c.shape[0]
    c_pad = jnp.zeros((SUBLANES, d), F32).at[:b].set(c)
    tn = 1024
    out = pl.pallas_call(
        _mod_kernel,
        grid=(depth, n // tn),
        in_specs=[
            pl.BlockSpec((SUBLANES, d), lambda l, j: (0, 0)),
            pl.BlockSpec((1, d, tn), lambda l, j: (l, 0, j)),
            pl.BlockSpec((1, 1, tn), lambda l, j: (l, 0, j)),
        ],
        out_specs=pl.BlockSpec((1, SUBLANES, tn), lambda l, j: (l, 0, j)),
        out_shape=jax.ShapeDtypeStruct((depth, SUBLANES, n), F32),
        compiler_params=_cparams(("arbitrary", "arbitrary")),
        name="adaln_mod",
    )(c_pad, ada_w, ada_b.reshape(depth, 1, n))
    mod = out[:, :b].reshape(depth, b, 9, d).transpose(0, 2, 1, 3)
    return mod.reshape(depth * 9 * b, 1, d)


def _mod_spec(layer, idx, batch, rows_per_batch_tiles):
    base = (layer * 9 + idx) * batch
    return pl.BlockSpec((None, 1, D_MODEL),
                        lambda m, *_: (base + m // rows_per_batch_tiles, 0, 0))


def _ffn_kernel(*refs, final_norm):
    if final_norm:
        (x_ref, sh_ref, sc_ref, gt_ref, g_ref, wg_ref, wu_ref, wo_ref, fg_ref,
         o_ref, h_scr, acc_scr) = refs
    else:
        (x_ref, sh_ref, sc_ref, gt_ref, g_ref, wg_ref, wu_ref, wo_ref,
         o_ref, h_scr, acc_scr) = refs
    f = pl.program_id(1)

    @pl.when(f == 0)
    def _():
        h = _adaln(x_ref[...], g_ref[...], sh_ref[...], sc_ref[...])
        h_scr[...] = h.astype(BF16)
        acc_scr[...] = jnp.zeros_like(acc_scr)

    h = h_scr[...]
    gate = _dot(h, wg_ref[...])
    up = _dot(h, wu_ref[...])
    a = (gate * _sigmoid(gate) * up).astype(BF16)
    acc_scr[...] += _dot(a, wo_ref[...])

    @pl.when(f == pl.num_programs(1) - 1)
    def _():
        out = x_ref[...] + (FFN_RES * (1.0 + gt_ref[...])) * acc_scr[...]
        if final_norm:
            ms = jnp.mean(out * out, axis=-1, keepdims=True)
            out = out * lax.rsqrt(ms + EPS) * fg_ref[...]
        o_ref[...] = out


def _ffn(x2, mod, layer, sub, batch, g, w_in, w_out, final_g=None):
    m, d = x2.shape
    f_dim = w_out.shape[0]
    tm, tf = FFN_TM, FFN_TF
    nf = f_dim // tf
    tiles_per_batch = (m // batch) // tm
    final_norm = final_g is not None
    row = lambda m_, f_: (m_, 0)
    const = lambda m_, f_: (0, 0)
    in_specs = [
        pl.BlockSpec((tm, d), row),
        _mod_spec(layer, sub * 3 + 0, batch, tiles_per_batch),
        _mod_spec(layer, sub * 3 + 1, batch, tiles_per_batch),
        _mod_spec(layer, sub * 3 + 2, batch, tiles_per_batch),
        pl.BlockSpec((1, d), const),
        pl.BlockSpec((d, tf), lambda m_, f_: (0, f_)),
        pl.BlockSpec((d, tf), lambda m_, f_: (0, nf + f_)),
        pl.BlockSpec((tf, d), lambda m_, f_: (f_, 0)),
    ]
    args = [x2, mod, mod, mod, g.reshape(1, d), w_in, w_in, w_out]
    if final_norm:
        in_specs.append(pl.BlockSpec((1, d), const))
        args.append(final_g.reshape(1, d))
    return pl.pallas_call(
        functools.partial(_ffn_kernel, final_norm=final_norm),
        grid=(m // tm, nf),
        in_specs=in_specs,
        out_specs=pl.BlockSpec((tm, d), row),
        out_shape=jax.ShapeDtypeStruct((m, d), F32),
        scratch_shapes=[pltpu.VMEM((tm, d), BF16), pltpu.VMEM((tm, d), F32)],
        compiler_params=_cparams(("arbitrary", "arbitrary")),
        name="ffn_final" if final_norm else "ffn",
    )(*args)


def _proj_kernel(x_ref, sh_ref, sc_ref, g_ref, w_ref, wgate_ref, wgate_t_ref, *out_refs,
                 widths):
    h = _adaln(x_ref[...], g_ref[...], sh_ref[...], sc_ref[...]).astype(BF16)
    main_refs = out_refs[:len(widths)]
    gate_ref, gate_t_ref = out_refs[len(widths):]
    col = 0
    for o_ref, width in zip(main_refs, widths):
        for c0 in range(0, width, PROJ_TN):
            o_ref[:, c0:c0 + PROJ_TN] = _dot(
                h, w_ref[:, col + c0:col + c0 + PROJ_TN]).astype(o_ref.dtype)
        col += width
    gate_ref[...] = _dot(h, wgate_ref[...])
    gate_t_ref[0] = _dot_nt(wgate_t_ref[...], h)


def _proj(x2, mod, layer, batch, g, w_main, w_gate, w_gate_t, widths):
    m, d = x2.shape
    tm = PROJ_TM
    seq = m // batch
    tiles_per_batch = seq // tm
    n_gate = w_gate_t.shape[0]
    row = lambda m_: (m_, 0)
    const = lambda m_: (0, 0)
    out_shape = [jax.ShapeDtypeStruct((m, w), BF16) for w in widths]
    out_specs = [pl.BlockSpec((tm, w), row) for w in widths]
    out_shape += [jax.ShapeDtypeStruct((m, LANES), F32),
                  jax.ShapeDtypeStruct((batch, n_gate, seq), F32)]
    out_specs += [pl.BlockSpec((tm, LANES), row),
                  pl.BlockSpec((1, n_gate, tm),
                               lambda m_: (m_ // tiles_per_batch, 0, m_ % tiles_per_batch))]
    return pl.pallas_call(
        functools.partial(_proj_kernel, widths=tuple(widths)),
        grid=(m // tm,),
        in_specs=[
            pl.BlockSpec((tm, d), row),
            _mod_spec(layer, 3, batch, tiles_per_batch),
            _mod_spec(layer, 4, batch, tiles_per_batch),
            pl.BlockSpec((1, d), const),
            pl.BlockSpec(w_main.shape, const),
            pl.BlockSpec(w_gate.shape, const),
            pl.BlockSpec(w_gate_t.shape, const),
        ],
        out_specs=out_specs,
        out_shape=out_shape,
        compiler_params=_cparams(("arbitrary",)),
        name="mixer_proj",
    )(x2, mod, mod, g.reshape(1, d), w_main, w_gate, w_gate_t)


def _outproj_kernel(a_ref, w_ref, x_ref, gt_ref, o_ref):
    y = _dot(a_ref[...], w_ref[...])
    o_ref[...] = x_ref[...] + (1.0 + gt_ref[...]) * y


def _outproj(a, w, x2, mod, layer, batch):
    m, d = x2.shape
    tm = PROJ_TM
    tiles_per_batch = (m // batch) // tm
    row = lambda m_: (m_, 0)
    return pl.pallas_call(
        _outproj_kernel,
        grid=(m // tm,),
        in_specs=[
            pl.BlockSpec((tm, a.shape[1]), row),
            pl.BlockSpec(w.shape, lambda m_: (0, 0)),
            pl.BlockSpec((tm, d), row),
            _mod_spec(layer, 5, batch, tiles_per_batch),
        ],
        out_specs=pl.BlockSpec((tm, d), row),
        out_shape=jax.ShapeDtypeStruct((m, d), F32),
        compiler_params=_cparams(("arbitrary",)),
        name="mixer_outproj",
    )(a, w, x2, mod)


def _mlstm_kernel(a_ref, v_ref, o_ref, gc_ref, gr_ref, cw_ref, bias_c_ref, bias_r_ref,
                  hg_ref, out_ref, xbuf, c_scr, m_scr):
    L = ML_CHUNK
    H = ML_HEADS
    c = pl.program_id(1)

    @pl.when(c == 0)
    def _():
        xbuf[0:SUBLANES, :] = jnp.zeros((SUBLANES, xbuf.shape[1]), F32)
        c_scr[...] = jnp.zeros_like(c_scr)
        m_scr[...] = jnp.zeros_like(m_scr)

    xbuf[SUBLANES:SUBLANES + L, :] = a_ref[...].astype(F32)
    y = cw_ref[CONV_W - 1:CONV_W, :] * xbuf[SUBLANES:SUBLANES + L, :]
    for j in range(CONV_W - 1):
        off = SUBLANES - (CONV_W - 1) + j
        y = y + cw_ref[j:j + 1, :] * xbuf[off:off + L, :]
    xbuf[0:SUBLANES, :] = xbuf[L:L + SUBLANES, :]
    act = y * _sigmoid(y)

    gcol = gc_ref[...] + bias_c_ref[...]
    grow = gr_ref[0] + bias_r_ref[...]
    lf_col = _log_sigmoid(gcol)
    lf_row = _log_sigmoid(grow)
    t_i = lax.broadcasted_iota(jnp.int32, (L, L), 0)
    s_i = lax.broadcasted_iota(jnp.int32, (L, L), 1)
    causal = t_i >= s_i
    tlow = jnp.where(causal, 1.0, 0.0).astype(BF16)
    tup = jnp.where(s_i >= t_i, 1.0, 0.0).astype(BF16)
    c_hi, c_mid, c_lo = _split3(lf_col)
    b_col = _dot(tlow, c_hi) + _dot(tlow, c_mid) + _dot(tlow, c_lo)
    r_hi, r_mid, r_lo = _split3(lf_row)
    b_row = _dot(r_hi, tup) + _dot(r_mid, tup) + _dot(r_lo, tup)
    g_tot = b_col[L - 1:L, :]

    lane = lax.broadcasted_iota(jnp.int32, (L, LANES), 1)
    first_half = lane < ML_DQK
    ones_col = jnp.where(lane == 0, 1.0, 0.0).astype(BF16)

    for h in range(H):
        a_h = act[:, h * LANES:(h + 1) * LANES]
        q = jnp.where(first_half, a_h, 0.0) * (ML_DQK ** -0.5)
        k = jnp.where(first_half, pltpu.roll(a_h, ML_DQK, 1), 0.0)
        qb = q.astype(BF16)
        kb = k.astype(BF16)
        v_aug = jnp.concatenate([v_ref[:, h * LANES:(h + 1) * LANES], ones_col], axis=1)

        bc = b_col[:, H + h:H + h + 1]
        ic = gcol[:, h:h + 1]
        br = b_row[H + h:H + h + 1, :]
        ir = grow[h:h + 1, :]
        g_h = g_tot[:, H + h:H + h + 1]
        m_prev = m_scr[h][0:1, 0:1]

        dmat = jnp.where(causal, bc + (ir - br), NEG)
        inter = bc + m_prev
        m_t = jnp.maximum(inter, jnp.max(dmat, axis=-1, keepdims=True))
        w_intra = jnp.exp(dmat - m_t)
        w_inter = jnp.exp(inter - m_t)
        s = _dot_nt(qb, kb) * w_intra
        c_prev = c_scr[h]
        num_aug = _dot(s.astype(BF16), v_aug) + w_inter * _dot(qb, c_prev.astype(BF16))
        num = num_aug[:, :ML_DV]
        den = num_aug[:, ML_DV:ML_DV + 1]
        h_t = num / jnp.maximum(jnp.abs(den), jnp.exp(-m_t))

        hn = h_t * lax.rsqrt(jnp.mean(h_t * h_t, axis=-1, keepdims=True) + EPS)
        hn = hn * hg_ref[:, h * ML_DV:(h + 1) * ML_DV]
        og = o_ref[:, h * ML_DV:(h + 1) * ML_DV].astype(F32)
        out_ref[:, h * ML_DV:(h + 1) * ML_DV] = (_sigmoid(og) * hn).astype(out_ref.dtype)

        a_w = g_h - bc + ic
        m_new = jnp.maximum(g_h + m_prev, jnp.max(a_w, axis=0, keepdims=True))
        wk = jnp.exp(a_w - m_new)
        decay = jnp.exp(g_h + m_prev - m_new)
        kw = (k * wk).astype(BF16)
        c_scr[h] = decay * c_prev + _dot_tn(kw, v_aug)
        m_scr[h] = jnp.broadcast_to(m_new, (SUBLANES, LANES))


def _mlstm(a_pre, v, o, gates, gates_t, conv_w, bias_col, bias_row, hnorm_g, batch):
    m, d = v.shape
    seq = m // batch
    L = ML_CHUNK
    nc = seq // L
    row = lambda b, c: (b * nc + c, 0)
    const = lambda b, c: (0, 0)
    return pl.pallas_call(
        _mlstm_kernel,
        grid=(batch, nc),
        in_specs=[
            pl.BlockSpec((L, d), row),
            pl.BlockSpec((L, d), row),
            pl.BlockSpec((L, d), row),
            pl.BlockSpec((L, LANES), row),
            pl.BlockSpec((1, 2 * ML_HEADS, L), lambda b, c: (b, 0, c)),
            pl.BlockSpec(conv_w.shape, const),
            pl.BlockSpec(bias_col.shape, const),
            pl.BlockSpec(bias_row.shape, const),
            pl.BlockSpec((1, d), const),
        ],
        out_specs=pl.BlockSpec((L, d), row),
        out_shape=jax.ShapeDtypeStruct((m, d), BF16),
        scratch_shapes=[
            pltpu.VMEM((SUBLANES + L, d), F32),
            pltpu.VMEM((ML_HEADS, LANES, 2 * LANES), F32),
            pltpu.VMEM((ML_HEADS, SUBLANES, LANES), F32),
        ],
        compiler_params=_cparams(("arbitrary", "arbitrary")),
        name="mlstm_scan",
    )(a_pre, v, o, gates, gates_t, conv_w, bias_col, bias_row, hnorm_g.reshape(1, d))


def _fox_prep_kernel(a_ref, v_ref, f_ref, bf_ref, gqk_ref, q_ref, k_ref, va_ref, carry):
    ts = FX_TS
    t = pl.program_id(1)

    @pl.when(t == 0)
    def _():
        carry[...] = jnp.zeros_like(carry)

    logf = _log_sigmoid(f_ref[...] + bf_ref[...])
    r_i = lax.broadcasted_iota(jnp.int32, (ts, ts), 0)
    c_i = lax.broadcasted_iota(jnp.int32, (ts, ts), 1)
    tlow = jnp.where(r_i >= c_i, 1.0, 0.0).astype(BF16)
    hi, mid, lo = _split3(logf)
    fcum = _dot(tlow, hi) + _dot(tlow, mid) + _dot(tlow, lo) + carry[0:1, :]
    carry[...] = jnp.broadcast_to(fcum[ts - 1:ts, :], carry.shape)

    lane = lax.broadcasted_iota(jnp.int32, (ts, LANES), 1)
    first_half = lane < FX_DH
    bl = FX_BIAS_LANE
    gqk = gqk_ref[...]
    for h in range(FX_HEADS):
        a = a_ref[:, h * LANES:(h + 1) * LANES].astype(F32)
        a2 = a * a
        sq_q = jnp.sum(jnp.where(first_half, a2, 0.0), axis=-1, keepdims=True)
        sq_k = jnp.sum(jnp.where(first_half, 0.0, a2), axis=-1, keepdims=True)
        r_q = lax.rsqrt(sq_q * (1.0 / FX_DH) + EPS)
        r_k = lax.rsqrt(sq_k * (1.0 / FX_DH) + EPS)
        a_n = a * jnp.where(first_half, r_q, r_k) * gqk
        fb = jnp.broadcast_to(fcum[:, h:h + 1], (ts, LANES))
        f_hi = fb.astype(BF16).astype(F32)
        f_r1 = fb - f_hi
        f_mid = f_r1.astype(BF16).astype(F32)
        f_lo = f_r1 - f_mid
        one = jnp.ones_like(fb)
        zero = jnp.zeros_like(fb)
        fq = jnp.where(lane == bl, f_hi, jnp.where(lane == bl + 1, f_mid, jnp.where(
            lane == bl + 2, f_lo, jnp.where(lane < bl + 6, one, zero))))
        fk = jnp.where(lane < bl + 3, one, jnp.where(lane == bl + 3, -f_hi, jnp.where(
            lane == bl + 4, -f_mid, jnp.where(lane == bl + 5, -f_lo, zero))))
        q_ref[:, h * LANES:(h + 1) * LANES] = jnp.where(first_half, a_n, fq).astype(BF16)
        k_ref[:, h * LANES:(h + 1) * LANES] = jnp.where(
            first_half, pltpu.roll(a_n, FX_DH, 1), fk).astype(BF16)

    ones_col = jnp.where(lane == FX_DH, 1.0, 0.0)
    for p in range(FX_HEADS // 2):
        vv = v_ref[:, p * LANES:(p + 1) * LANES].astype(F32)
        va_ref[:, (2 * p) * LANES:(2 * p + 1) * LANES] = jnp.where(
            first_half, vv, ones_col).astype(BF16)
        va_ref[:, (2 * p + 1) * LANES:(2 * p + 2) * LANES] = jnp.where(
            first_half, pltpu.roll(vv, FX_DH, 1), ones_col).astype(BF16)


def _fox_prep(a, v, fgate, bf_row, gqk, batch):
    m = a.shape[0]
    seq = m // batch
    ts = FX_TS
    nt = seq // ts
    wide = FX_HEADS * LANES
    row = lambda b, t: (b * nt + t, 0)
    const = lambda b, t: (0, 0)
    return pl.pallas_call(
        _fox_prep_kernel,
        grid=(batch, nt),
        in_specs=[
            pl.BlockSpec((ts, wide), row),
            pl.BlockSpec((ts, FX_HEADS * FX_DH), row),
            pl.BlockSpec((ts, LANES), row),
            pl.BlockSpec((1, LANES), const),
            pl.BlockSpec((1, LANES), const),
        ],
        out_specs=[pl.BlockSpec((ts, wide), row)] * 3,
        out_shape=[jax.ShapeDtypeStruct((m, wide), BF16)] * 3,
        scratch_shapes=[pltpu.VMEM((SUBLANES, LANES), F32)],
        compiler_params=_cparams(("arbitrary", "arbitrary")),
        name="fox_prep",
    )(a, v, fgate, bf_row, gqk)


def _fox_attn_kernel(q_ref, k_ref, v_ref, og_ref, o_ref, m_scr, acc_scr):
    tq, tk = FX_TQ, FX_TK
    qi = pl.program_id(2)
    r_i = lax.broadcasted_iota(jnp.int32, (tq, tk), 0)
    c_i = lax.broadcasted_iota(jnp.int32, (tq, tk), 1)
    causal = r_i >= c_i

    for hh in range(2):
        q = q_ref[0, :, hh * LANES:(hh + 1) * LANES]
        m_scr[...] = jnp.full(m_scr.shape, NEG, F32)
        acc_scr[...] = jnp.zeros_like(acc_scr)

        def step(kj, masked):
            start = pl.multiple_of(kj * tk, tk)
            k = k_ref[0, pl.ds(start, tk), hh * LANES:(hh + 1) * LANES]
            v = v_ref[0, pl.ds(start, tk), hh * LANES:(hh + 1) * LANES]
            s = _dot_nt(q, k)
            if masked:
                s = jnp.where(causal, s, NEG)
            m_prev = m_scr[...]
            m_new = jnp.maximum(m_prev, jnp.max(s, axis=-1, keepdims=True))
            p = jnp.exp(s - m_new[:, 0:1])
            corr = jnp.exp(m_prev - m_new)
            acc_scr[...] = corr * acc_scr[...] + _dot(p.astype(BF16), v)
            m_scr[...] = m_new

        def body(kj, carry):
            step(kj, False)
            return carry

        lax.fori_loop(0, qi, body, 0)
        step(qi, True)

        acc = acc_scr[...]
        out = acc[:, :FX_DH] / acc[:, FX_DH:FX_DH + 1]
        og = og_ref[0, :, hh * FX_DH:(hh + 1) * FX_DH].astype(F32)
        o_ref[0, :, hh * FX_DH:(hh + 1) * FX_DH] = (_sigmoid(og) * out).astype(o_ref.dtype)


def _fox_attn(q_aug, k_aug, v_aug, og, batch):
    m, wide = q_aug.shape
    seq = m // batch
    tq = FX_TQ
    nq = seq // tq
    pairs = FX_HEADS // 2
    q3 = q_aug.reshape(batch, seq, wide)
    k3 = k_aug.reshape(batch, seq, wide)
    v3 = v_aug.reshape(batch, seq, wide)
    og3 = og.reshape(batch, seq, FX_HEADS * FX_DH)
    out = pl.pallas_call(
        _fox_attn_kernel,
        grid=(batch, pairs, nq),
        in_specs=[
            pl.BlockSpec((1, tq, 2 * LANES), lambda b, p, i: (b, i, p)),
            pl.BlockSpec((1, seq, 2 * LANES), lambda b, p, i: (b, 0, p)),
            pl.BlockSpec((1, seq, 2 * LANES), lambda b, p, i: (b, 0, p)),
            pl.BlockSpec((1, tq, LANES), lambda b, p, i: (b, i, p)),
        ],
        out_specs=pl.BlockSpec((1, tq, LANES), lambda b, p, i: (b, i, p)),
        out_shape=jax.ShapeDtypeStruct((batch, seq, FX_HEADS * FX_DH), BF16),
        scratch_shapes=[pltpu.VMEM((tq, LANES), F32), pltpu.VMEM((tq, LANES), F32)],
        compiler_params=_cparams(("arbitrary", "arbitrary", "arbitrary")),
        name="fox_attn",
    )(q3, k3, v3, og3)
    return out.reshape(m, FX_HEADS * FX_DH)


def _pair_heads(wq, wk, heads, dh):
    lead = wq.shape[0]
    return jnp.concatenate(
        [wq.reshape(lead, heads, dh), wk.reshape(lead, heads, dh)], axis=2
    ).reshape(lead, heads * 2 * dh)


def _pad_lanes(w):
    return jnp.zeros((w.shape[0], LANES), w.dtype).at[:, :w.shape[1]].set(w)


def kernel(x, c, ada_w, ada_b, norm_g, ffn_w_in, ffn_w_out, ml_w_in, ml_conv_w, ml_b_i, ml_b_f,
           ml_hnorm_g, ml_w_out, fx_w_in, fx_b_f, fx_qnorm_g, fx_knorm_g, fx_w_out, final_g):
    batch, seq, d = x.shape
    assert d == D_MODEL and seq % FFN_TM == 0 and seq % FX_TQ == 0 and seq % ML_CHUNK == 0
    depth = ada_w.shape[0]
    assert depth == 2 and batch <= SUBLANES

    mod = _modulation(c, ada_w, ada_b)
    w_in_b = ffn_w_in.astype(BF16)
    w_out_b = ffn_w_out.astype(BF16)
    x2 = x.reshape(batch * seq, d)

    x2 = _ffn(x2, mod, 0, 0, batch, norm_g[0, 0], w_in_b[0, 0], w_out_b[0, 0])

    H, dk, dv = ML_HEADS, ML_DQK, ML_DV
    w = ml_w_in[0]
    n_qk = H * dk
    w_a = _pair_heads(w[:, :n_qk], w[:, n_qk:2 * n_qk], H, dk)
    w_v = w[:, 2 * n_qk:2 * n_qk + H * dv]
    w_o = w[:, 2 * n_qk + H * dv:2 * n_qk + 2 * H * dv]
    w_g = w[:, 2 * n_qk + 2 * H * dv:]
    w_main = jnp.concatenate([w_a, w_v, w_o], axis=1).astype(BF16)
    cw = ml_conv_w[0]
    conv_w = _pair_heads(cw[:, :n_qk], cw[:, n_qk:], H, dk)
    bias = jnp.concatenate([ml_b_i[0], ml_b_f[0]])
    bias_col = _pad_lanes(bias.reshape(1, 2 * H))
    bias_row = bias.reshape(2 * H, 1)
    a_pre, v, o, gates, gates_t = _proj(
        x2, mod, 0, batch, norm_g[0, 1], w_main, _pad_lanes(w_g).astype(BF16),
        w_g.T.astype(BF16), (2 * n_qk, H * dv, H * dv))
    hs = _mlstm(a_pre, v, o, gates, gates_t, conv_w, bias_col, bias_row, ml_hnorm_g[0], batch)
    x2 = _outproj(hs, ml_w_out[0].astype(BF16), x2, mod, 0, batch)

    x2 = _ffn(x2, mod, 0, 2, batch, norm_g[0, 2], w_in_b[0, 1], w_out_b[0, 1])

    x2 = _ffn(x2, mod, 1, 0, batch, norm_g[1, 0], w_in_b[1, 0], w_out_b[1, 0])

    H, dh = FX_HEADS, FX_DH
    w = fx_w_in[0]
    n = H * dh
    w_a = _pair_heads(w[:, :n], w[:, n:2 * n], H, dh)
    w_main = jnp.concatenate([w_a, w[:, 2 * n:4 * n]], axis=1).astype(BF16)
    w_f = w[:, 4 * n:]
    a, v, og, fgate, _ = _proj(
        x2, mod, 1, batch, norm_g[1, 1], w_main, _pad_lanes(w_f).astype(BF16),
        w_f.T.astype(BF16), (2 * n, n, n))
    gqk = jnp.concatenate([fx_qnorm_g[0] * (dh ** -0.5), fx_knorm_g[0]]).reshape(1, LANES)
    bf_row = _pad_lanes(fx_b_f[0].reshape(1, H))
    q_aug, k_aug, v_aug = _fox_prep(a, v, fgate, bf_row, gqk, batch)
    attn = _fox_attn(q_aug, k_aug, v_aug, og, batch)
    x2 = _outproj(attn, fx_w_out[0].astype(BF16), x2, mod, 1, batch)

    x2 = _ffn(x2, mod, 1, 2, batch, norm_g[1, 2], w_in_b[1, 1], w_out_b[1, 1], final_g=final_g)
    return x2.reshape(batch, seq, d)
```

```python
import functools

import jax
import jax.numpy as jnp
from jax import lax
from jax.experimental import pallas as pl
from jax.experimental.pallas import tpu as pltpu

D_MODEL = 1024
D_FF = 4 * D_MODEL
EPS = 1e-6
FFN_RES = 0.5
CONV_W = 4
ML_HEADS = 8
ML_DV = 128
ML_DQK = 64
FX_HEADS = 16
FX_DH = 64
NEG = -1e30

LANES = 128
SUBLANES = 8
VMEM_LIMIT_BYTES = 56 * 1024 * 1024

FFN_TM = 1024
FFN_TF = 512
PROJ_TM = 512
PROJ_TN = 512
ML_CHUNK = 128
FX_TS = 512
FX_TQ = 512
FX_TK = 512
FX_BIAS_LANE = FX_DH
FX_ONES_ROWS = 16
LOG2E = 1.4426950408889634

BF16 = jnp.bfloat16
F32 = jnp.float32


def _cparams(sem):
    return pltpu.CompilerParams(dimension_semantics=sem, vmem_limit_bytes=VMEM_LIMIT_BYTES)


def _dot(a, b):
    return jnp.dot(a, b, preferred_element_type=F32)


def _dot_nt(a, b):
    return lax.dot_general(a, b, (((1,), (1,)), ((), ())), preferred_element_type=F32)


def _dot_tn(a, b):
    return lax.dot_general(a, b, (((0,), (0,)), ((), ())), preferred_element_type=F32)


def _sigmoid(x):
    return 1.0 / (1.0 + jnp.exp(-x))


def _log_sigmoid(x):
    return jnp.minimum(x, 0.0) - jnp.log(1.0 + jnp.exp(-jnp.abs(x)))


def _split3(x):
    hi = x.astype(BF16)
    r1 = x - hi.astype(F32)
    mid = r1.astype(BF16)
    lo = (r1 - mid.astype(F32)).astype(BF16)
    return hi, mid, lo


def _adaln(x, g, shift, scale):
    ms = jnp.mean(x * x, axis=-1, keepdims=True)
    return (x * lax.rsqrt(ms + EPS) * g) * (1.0 + scale) + shift


def _mod_kernel(c_ref, w_ref, b_ref, o_ref):
    c = c_ref[...]
    cond = (c * _sigmoid(c)).astype(BF16)
    o_ref[0] = _dot(cond, w_ref[0].astype(BF16)) + b_ref[0]


def _modulation(c, ada_w, ada_b):
    depth, d, n = ada_w.shape
    b = c.shape[0]
    c_pad = jnp.zeros((SUBLANES, d), F32).at[:b].set(c)
    tn = 1024
    out = pl.pallas_call(
        _mod_kernel,
        grid=(depth, n // tn),
        in_specs=[
            pl.BlockSpec((SUBLANES, d), lambda l, j: (0, 0)),
            pl.BlockSpec((1, d, tn), lambda l, j: (l, 0, j)),
            pl.BlockSpec((1, 1, tn), lambda l, j: (l, 0, j)),
        ],
        out_specs=pl.BlockSpec((1, SUBLANES, tn), lambda l, j: (l, 0, j)),
        out_shape=jax.ShapeDtypeStruct((depth, SUBLANES, n), F32),
        compiler_params=_cparams(("arbitrary", "arbitrary")),
        name="adaln_mod",
    )(c_pad, ada_w, ada_b.reshape(depth, 1, n))
    mod = out[:, :b].reshape(depth, b, 9, d).transpose(0, 2, 1, 3)
    return mod.reshape(depth * 9 * b, 1, d)


def _mod_spec(layer, idx, batch, rows_per_batch_tiles):
    base = (layer * 9 + idx) * batch
    return pl.BlockSpec((None, 1, D_MODEL),
                        lambda m, *_: (base + m // rows_per_batch_tiles, 0, 0))


def _ffn_kernel(*refs, final_norm):
    if final_norm:
        (x_ref, sh_ref, sc_ref, gt_ref, g_ref, wg_ref, wu_ref, wo_ref, fg_ref,
         o_ref, h_scr, acc_scr) = refs
    else:
        (x_ref, sh_ref, sc_ref, gt_ref, g_ref, wg_ref, wu_ref, wo_ref,
         o_ref, h_scr, acc_scr) = refs
    f = pl.program_id(1)

    @pl.when(f == 0)
    def _():
        h = _adaln(x_ref[...], g_ref[...], sh_ref[...], sc_ref[...])
        h_scr[...] = h.astype(BF16)
        acc_scr[...] = jnp.zeros_like(acc_scr)

    h = h_scr[...]
    gate = _dot(h, wg_ref[...])
    up = _dot(h, wu_ref[...])
    a = (gate * _sigmoid(gate) * up).astype(BF16)
    acc_scr[...] += _dot(a, wo_ref[...])

    @pl.when(f == pl.num_programs(1) - 1)
    def _():
        out = x_ref[...] + (FFN_RES * (1.0 + gt_ref[...])) * acc_scr[...]
        if final_norm:
            ms = jnp.mean(out * out, axis=-1, keepdims=True)
            out = out * lax.rsqrt(ms + EPS) * fg_ref[...]
        o_ref[...] = out


def _ffn(x2, mod, layer, sub, batch, g, w_in, w_out, final_g=None):
    m, d = x2.shape
    f_dim = w_out.shape[0]
    tm, tf = FFN_TM, FFN_TF
    nf = f_dim // tf
    tiles_per_batch = (m // batch) // tm
    final_norm = final_g is not None
    row = lambda m_, f_: (m_, 0)
    const = lambda m_, f_: (0, 0)
    in_specs = [
        pl.BlockSpec((tm, d), row),
        _mod_spec(layer, sub * 3 + 0, batch, tiles_per_batch),
        _mod_spec(layer, sub * 3 + 1, batch, tiles_per_batch),
        _mod_spec(layer, sub * 3 + 2, batch, tiles_per_batch),
        pl.BlockSpec((1, d), const),
        pl.BlockSpec((d, tf), lambda m_, f_: (0, f_)),
        pl.BlockSpec((d, tf), lambda m_, f_: (0, nf + f_)),
        pl.BlockSpec((tf, d), lambda m_, f_: (f_, 0)),
    ]
    args = [x2, mod, mod, mod, g.reshape(1, d), w_in, w_in, w_out]
    if final_norm:
        in_specs.append(pl.BlockSpec((1, d), const))
        args.append(final_g.reshape(1, d))
    return pl.pallas_call(
        functools.partial(_ffn_kernel, final_norm=final_norm),
        grid=(m // tm, nf),
        in_specs=in_specs,
        out_specs=pl.BlockSpec((tm, d), row),
        out_shape=jax.ShapeDtypeStruct((m, d), F32),
        scratch_shapes=[pltpu.VMEM((tm, d), BF16), pltpu.VMEM((tm, d), F32)],
        compiler_params=_cparams(("arbitrary", "arbitrary")),
        name="ffn_final" if final_norm else "ffn",
    )(*args)


def _proj_kernel(x_ref, sh_ref, sc_ref, g_ref, w_ref, wgate_ref, wt_ref, *out_refs, widths):
    h = _adaln(x_ref[...], g_ref[...], sh_ref[...], sc_ref[...]).astype(BF16)
    main_refs = out_refs[:len(widths)]
    gate_ref, t_ref = out_refs[len(widths):]
    col = 0
    for o_ref, width in zip(main_refs, widths):
        for c0 in range(0, width, PROJ_TN):
            o_ref[:, c0:c0 + PROJ_TN] = _dot(
                h, w_ref[:, col + c0:col + c0 + PROJ_TN]).astype(o_ref.dtype)
        col += width
    gate_ref[...] = _dot(h, wgate_ref[...])
    n_t = wt_ref.shape[0]
    step = min(n_t, PROJ_TN)
    for r0 in range(0, n_t, step):
        t_ref[0, r0:r0 + step, :] = _dot_nt(wt_ref[r0:r0 + step, :], h).astype(t_ref.dtype)


def _proj(x2, mod, layer, batch, g, w_main, w_gate, w_t, widths, t_dtype):
    m, d = x2.shape
    tm = PROJ_TM
    seq = m // batch
    tiles_per_batch = seq // tm
    n_t = w_t.shape[0]
    row = lambda m_: (m_, 0)
    const = lambda m_: (0, 0)
    out_shape = [jax.ShapeDtypeStruct((m, w), BF16) for w in widths]
    out_specs = [pl.BlockSpec((tm, w), row) for w in widths]
    out_shape += [jax.ShapeDtypeStruct((m, LANES), F32),
                  jax.ShapeDtypeStruct((m // tm, n_t, tm), t_dtype)]
    out_specs += [pl.BlockSpec((tm, LANES), row),
                  pl.BlockSpec((1, n_t, tm), lambda m_: (m_, 0, 0))]
    return pl.pallas_call(
        functools.partial(_proj_kernel, widths=tuple(widths)),
        grid=(m // tm,),
        in_specs=[
            pl.BlockSpec((tm, d), row),
            _mod_spec(layer, 3, batch, tiles_per_batch),
            _mod_spec(layer, 4, batch, tiles_per_batch),
            pl.BlockSpec((1, d), const),
            pl.BlockSpec(w_main.shape, const),
            pl.BlockSpec(w_gate.shape, const),
            pl.BlockSpec(w_t.shape, const),
        ],
        out_specs=out_specs,
        out_shape=out_shape,
        compiler_params=_cparams(("arbitrary",)),
        name="mixer_proj",
    )(x2, mod, mod, g.reshape(1, d), w_main, w_gate, w_t)


def _outproj_kernel(a_ref, w_ref, x_ref, gt_ref, o_ref):
    y = _dot(a_ref[...], w_ref[...])
    o_ref[...] = x_ref[...] + (1.0 + gt_ref[...]) * y


def _outproj(a, w, x2, mod, layer, batch):
    m, d = x2.shape
    tm = PROJ_TM
    tiles_per_batch = (m // batch) // tm
    row = lambda m_: (m_, 0)
    return pl.pallas_call(
        _outproj_kernel,
        grid=(m // tm,),
        in_specs=[
            pl.BlockSpec((tm, a.shape[1]), row),
            pl.BlockSpec(w.shape, lambda m_: (0, 0)),
            pl.BlockSpec((tm, d), row),
            _mod_spec(layer, 5, batch, tiles_per_batch),
        ],
        out_specs=pl.BlockSpec((tm, d), row),
        out_shape=jax.ShapeDtypeStruct((m, d), F32),
        compiler_params=_cparams(("arbitrary",)),
        name="mixer_outproj",
    )(a, w, x2, mod)


def _mlstm_kernel(a_ref, v_ref, o_ref, gc_ref, gr_ref, cw_ref, bias_c_ref, bias_r_ref,
                  hg_ref, out_ref, xbuf, c_scr, m_scr):
    L = ML_CHUNK
    H = ML_HEADS
    c = pl.program_id(1)

    @pl.when(c == 0)
    def _():
        xbuf[0:SUBLANES, :] = jnp.zeros((SUBLANES, xbuf.shape[1]), F32)
        c_scr[...] = jnp.zeros_like(c_scr)
        m_scr[...] = jnp.zeros_like(m_scr)

    xbuf[SUBLANES:SUBLANES + L, :] = a_ref[...].astype(F32)
    y = cw_ref[CONV_W - 1:CONV_W, :] * xbuf[SUBLANES:SUBLANES + L, :]
    for j in range(CONV_W - 1):
        off = SUBLANES - (CONV_W - 1) + j
        y = y + cw_ref[j:j + 1, :] * xbuf[off:off + L, :]
    xbuf[0:SUBLANES, :] = xbuf[L:L + SUBLANES, :]
    act = y * _sigmoid(y)

    gcol = gc_ref[...] + bias_c_ref[...]
    grow = gr_ref[0] + bias_r_ref[...]
    lf_col = _log_sigmoid(gcol)
    lf_row = _log_sigmoid(grow)
    t_i = lax.broadcasted_iota(jnp.int32, (L, L), 0)
    s_i = lax.broadcasted_iota(jnp.int32, (L, L), 1)
    causal = t_i >= s_i
    tlow = jnp.where(causal, 1.0, 0.0).astype(BF16)
    tup = jnp.where(s_i >= t_i, 1.0, 0.0).astype(BF16)
    c_hi, c_mid, c_lo = _split3(lf_col)
    b_col = _dot(tlow, c_hi) + _dot(tlow, c_mid) + _dot(tlow, c_lo)
    r_hi, r_mid, r_lo = _split3(lf_row)
    b_row = _dot(r_hi, tup) + _dot(r_mid, tup) + _dot(r_lo, tup)
    g_tot = b_col[L - 1:L, :]

    lane = lax.broadcasted_iota(jnp.int32, (L, LANES), 1)
    first_half = lane < ML_DQK
    ones_col = jnp.where(lane == 0, 1.0, 0.0).astype(BF16)

    for h in range(H):
        a_h = act[:, h * LANES:(h + 1) * LANES]
        q = jnp.where(first_half, a_h, 0.0) * (ML_DQK ** -0.5)
        k = jnp.where(first_half, pltpu.roll(a_h, ML_DQK, 1), 0.0)
        qb = q.astype(BF16)
        kb = k.astype(BF16)
        v_aug = jnp.concatenate([v_ref[:, h * LANES:(h + 1) * LANES], ones_col], axis=1)

        bc = b_col[:, H + h:H + h + 1]
        ic = gcol[:, h:h + 1]
        br = b_row[H + h:H + h + 1, :]
        ir = grow[h:h + 1, :]
        g_h = g_tot[:, H + h:H + h + 1]
        m_prev = m_scr[h][0:1, 0:1]

        dmat = jnp.where(causal, bc + (ir - br), NEG)
        inter = bc + m_prev
        m_t = jnp.maximum(inter, jnp.max(dmat, axis=-1, keepdims=True))
        w_intra = jnp.exp(dmat - m_t)
        w_inter = jnp.exp(inter - m_t)
        s = _dot_nt(qb, kb) * w_intra
        c_prev = c_scr[h]
        num_aug = _dot(s.astype(BF16), v_aug) + w_inter * _dot(qb, c_prev.astype(BF16))
        num = num_aug[:, :ML_DV]
        den = num_aug[:, ML_DV:ML_DV + 1]
        h_t = num / jnp.maximum(jnp.abs(den), jnp.exp(-m_t))

        hn = h_t * lax.rsqrt(jnp.mean(h_t * h_t, axis=-1, keepdims=True) + EPS)
        hn = hn * hg_ref[:, h * ML_DV:(h + 1) * ML_DV]
        og = o_ref[:, h * ML_DV:(h + 1) * ML_DV].astype(F32)
        out_ref[:, h * ML_DV:(h + 1) * ML_DV] = (_sigmoid(og) * hn).astype(out_ref.dtype)

        a_w = g_h - bc + ic
        m_new = jnp.maximum(g_h + m_prev, jnp.max(a_w, axis=0, keepdims=True))
        wk = jnp.exp(a_w - m_new)
        decay = jnp.exp(g_h + m_prev - m_new)
        kw = (k * wk).astype(BF16)
        c_scr[h] = decay * c_prev + _dot_tn(kw, v_aug)
        m_scr[h] = jnp.broadcast_to(m_new, (SUBLANES, LANES))


def _mlstm(a_pre, v, o, gates, gates_t, conv_w, bias_col, bias_row, hnorm_g, batch):
    m, d = v.shape
    seq = m // batch
    L = ML_CHUNK
    nc = seq // L
    row = lambda b, c: (b * nc + c, 0)
    const = lambda b, c: (0, 0)
    return pl.pallas_call(
        _mlstm_kernel,
        grid=(batch, nc),
        in_specs=[
            pl.BlockSpec((L, d), row),
            pl.BlockSpec((L, d), row),
            pl.BlockSpec((L, d), row),
            pl.BlockSpec((L, LANES), row),
            pl.BlockSpec((1, 2 * ML_HEADS, L),
                         lambda b, c: ((b * nc + c) // (PROJ_TM // L), 0, (b * nc + c) % (PROJ_TM // L))),
            pl.BlockSpec(conv_w.shape, const),
            pl.BlockSpec(bias_col.shape, const),
            pl.BlockSpec(bias_row.shape, const),
            pl.BlockSpec((1, d), const),
        ],
        out_specs=pl.BlockSpec((L, d), row),
        out_shape=jax.ShapeDtypeStruct((m, d), BF16),
        scratch_shapes=[
            pltpu.VMEM((SUBLANES + L, d), F32),
            pltpu.VMEM((ML_HEADS, LANES, 2 * LANES), F32),
            pltpu.VMEM((ML_HEADS, SUBLANES, LANES), F32),
        ],
        compiler_params=_cparams(("arbitrary", "arbitrary")),
        name="mlstm_scan",
    )(a_pre, v, o, gates, gates_t, conv_w, bias_col, bias_row, hnorm_g.reshape(1, d))


def _fox_prep_kernel(a_ref, f_ref, bf_ref, gqk_ref, q_ref, k_ref, carry):
    ts = FX_TS
    t = pl.program_id(1)

    @pl.when(t == 0)
    def _():
        carry[...] = jnp.zeros_like(carry)

    logf = _log_sigmoid(f_ref[...] + bf_ref[...]) * LOG2E
    r_i = lax.broadcasted_iota(jnp.int32, (ts, ts), 0)
    c_i = lax.broadcasted_iota(jnp.int32, (ts, ts), 1)
    tlow = jnp.where(r_i >= c_i, 1.0, 0.0).astype(BF16)
    hi, mid, lo = _split3(logf)
    fcum = _dot(tlow, hi) + _dot(tlow, mid) + _dot(tlow, lo) + carry[0:1, :]
    carry[...] = jnp.broadcast_to(fcum[ts - 1:ts, :], carry.shape)

    lane = lax.broadcasted_iota(jnp.int32, (ts, LANES), 1)
    first_half = lane < FX_DH
    bl = FX_BIAS_LANE
    gqk = gqk_ref[...]
    for h in range(FX_HEADS):
        a = a_ref[:, h * LANES:(h + 1) * LANES].astype(F32)
        a2 = a * a
        sq_q = jnp.sum(jnp.where(first_half, a2, 0.0), axis=-1, keepdims=True)
        sq_k = jnp.sum(jnp.where(first_half, 0.0, a2), axis=-1, keepdims=True)
        r_q = lax.rsqrt(sq_q * (1.0 / FX_DH) + EPS)
        r_k = lax.rsqrt(sq_k * (1.0 / FX_DH) + EPS)
        a_n = a * jnp.where(first_half, r_q, r_k) * gqk
        fb = jnp.broadcast_to(fcum[:, h:h + 1], (ts, LANES))
        f_hi = fb.astype(BF16).astype(F32)
        f_r1 = fb - f_hi
        f_mid = f_r1.astype(BF16).astype(F32)
        f_lo = f_r1 - f_mid
        one = jnp.ones_like(fb)
        zero = jnp.zeros_like(fb)
        fq = jnp.where(lane == bl, f_hi, jnp.where(lane == bl + 1, f_mid, jnp.where(
            lane == bl + 2, f_lo, jnp.where(lane < bl + 6, one, zero))))
        fk = jnp.where(lane < bl + 3, one, jnp.where(lane == bl + 3, -f_hi, jnp.where(
            lane == bl + 4, -f_mid, jnp.where(lane == bl + 5, -f_lo, zero))))
        q_ref[:, h * LANES:(h + 1) * LANES] = jnp.where(first_half, a_n, fq).astype(BF16)
        k_ref[:, h * LANES:(h + 1) * LANES] = jnp.where(
            first_half, pltpu.roll(a_n, FX_DH, 1), fk).astype(BF16)


def _fox_prep(a, fgate, bf_row, gqk, batch):
    m = a.shape[0]
    seq = m // batch
    ts = FX_TS
    nt = seq // ts
    wide = FX_HEADS * LANES
    row = lambda b, t: (b * nt + t, 0)
    const = lambda b, t: (0, 0)
    return pl.pallas_call(
        _fox_prep_kernel,
        grid=(batch, nt),
        in_specs=[
            pl.BlockSpec((ts, wide), row),
            pl.BlockSpec((ts, LANES), row),
            pl.BlockSpec((1, LANES), const),
            pl.BlockSpec((1, LANES), const),
        ],
        out_specs=[pl.BlockSpec((ts, wide), row)] * 2,
        out_shape=[jax.ShapeDtypeStruct((m, wide), BF16)] * 2,
        scratch_shapes=[pltpu.VMEM((SUBLANES, LANES), F32)],
        compiler_params=_cparams(("arbitrary", "arbitrary")),
        name="fox_prep",
    )(a, fgate, bf_row, gqk)


def _fox_attn_kernel(q_ref, k_ref, vt_ref, og_ref, o_ref, m_scr, acc_scr,
                     sa_scr, mxa_scr, sb_scr, mxb_scr):
    tq, tk = FX_TQ, FX_TK
    qi = pl.program_id(2)
    key_i = lax.broadcasted_iota(jnp.int32, (tk, tq), 0)
    qry_i = lax.broadcasted_iota(jnp.int32, (tk, tq), 1)
    causal = key_i <= qry_i
    ones_rows = jnp.ones((FX_ONES_ROWS, tk), BF16)

    m_scr[...] = jnp.full(m_scr.shape, NEG, F32)
    acc_scr[...] = jnp.zeros_like(acc_scr)

    def score(kj, s_ref, mx_ref):
        start = pl.multiple_of(kj * tk, tk)
        for hh in range(2):
            q = q_ref[0, :, hh * LANES:(hh + 1) * LANES]
            k = k_ref[0, pl.ds(start, tk), hh * LANES:(hh + 1) * LANES]
            s = _dot_nt(k, q)
            s_ref[hh] = s
            mx_ref[hh] = jnp.max(s, axis=0, keepdims=True)

    def consume(kj, s_ref, mx_ref, masked):
        for hh in range(2):
            s = s_ref[hh]
            if masked:
                s = jnp.where(causal, s, NEG)
                s_max = jnp.max(s, axis=0, keepdims=True)
            else:
                s_max = mx_ref[hh]
            m_prev = m_scr[hh]
            m_new = jnp.maximum(m_prev, s_max)
            p = jnp.exp2(s - m_new).astype(BF16)
            corr = jnp.exp2(m_prev - m_new)
            vt = jnp.concatenate(
                [vt_ref[0, kj, hh * FX_DH:(hh + 1) * FX_DH, :], ones_rows], axis=0)
            acc_scr[hh] = corr * acc_scr[hh] + _dot(vt, p)
            m_scr[hh] = m_new

    def body(i, carry):
        kj = 2 * i
        score(kj + 1, sb_scr, mxb_scr)
        consume(kj, sa_scr, mxa_scr, False)
        score(kj + 2, sa_scr, mxa_scr)
        consume(kj + 1, sb_scr, mxb_scr, False)
        return carry

    score(0, sa_scr, mxa_scr)
    lax.fori_loop(0, qi // 2, body, 0)

    @pl.when(qi % 2 == 0)
    def _():
        consume(qi, sa_scr, mxa_scr, True)

    @pl.when(qi % 2 == 1)
    def _():
        score(qi, sb_scr, mxb_scr)
        consume(qi - 1, sa_scr, mxa_scr, False)
        consume(qi, sb_scr, mxb_scr, True)

    outs = []
    for hh in range(2):
        acc = acc_scr[hh]
        outs.append(acc[:FX_DH, :] / acc[FX_DH:FX_DH + 1, :])
    out = jnp.concatenate(outs, axis=0).T
    og = og_ref[0].astype(F32)
    o_ref[0] = (_sigmoid(og) * out).astype(o_ref.dtype)


def _fox_attn(q_aug, k_aug, vt, og, batch):
    m, wide = q_aug.shape
    seq = m // batch
    tq, tk = FX_TQ, FX_TK
    assert tq == tk == PROJ_TM
    nq = seq // tq
    pairs = FX_HEADS // 2
    q3 = q_aug.reshape(batch, seq, wide)
    k3 = k_aug.reshape(batch, seq, wide)
    vt4 = vt.reshape(batch, seq // tk, FX_HEADS * FX_DH, tk)
    og3 = og.reshape(batch, seq, FX_HEADS * FX_DH)
    out = pl.pallas_call(
        _fox_attn_kernel,
        grid=(batch, pairs, nq),
        in_specs=[
            pl.BlockSpec((1, tq, 2 * LANES), lambda b, p, i: (b, i, p)),
            pl.BlockSpec((1, seq, 2 * LANES), lambda b, p, i: (b, 0, p)),
            pl.BlockSpec((1, seq // tk, 2 * FX_DH, tk), lambda b, p, i: (b, 0, p, 0)),
            pl.BlockSpec((1, tq, LANES), lambda b, p, i: (b, i, p)),
        ],
        out_specs=pl.BlockSpec((1, tq, LANES), lambda b, p, i: (b, i, p)),
        out_shape=jax.ShapeDtypeStruct((batch, seq, FX_HEADS * FX_DH), BF16),
        scratch_shapes=[pltpu.VMEM((2, 1, tq), F32),
                        pltpu.VMEM((2, FX_DH + FX_ONES_ROWS, tq), F32),
                        pltpu.VMEM((2, tk, tq), F32), pltpu.VMEM((2, 1, tq), F32),
                        pltpu.VMEM((2, tk, tq), F32), pltpu.VMEM((2, 1, tq), F32)],
        compiler_params=_cparams(("arbitrary", "arbitrary", "arbitrary")),
        name="fox_attn",
    )(q3, k3, vt4, og3)
    return out.reshape(m, FX_HEADS * FX_DH)


def _pair_heads(wq, wk, heads, dh):
    lead = wq.shape[0]
    return jnp.concatenate(
        [wq.reshape(lead, heads, dh), wk.reshape(lead, heads, dh)], axis=2
    ).reshape(lead, heads * 2 * dh)


def _pad_lanes(w):
    return jnp.zeros((w.shape[0], LANES), w.dtype).at[:, :w.shape[1]].set(w)


def kernel(x, c, ada_w, ada_b, norm_g, ffn_w_in, ffn_w_out, ml_w_in, ml_conv_w, ml_b_i, ml_b_f,
           ml_hnorm_g, ml_w_out, fx_w_in, fx_b_f, fx_qnorm_g, fx_knorm_g, fx_w_out, final_g):
    batch, seq, d = x.shape
    assert d == D_MODEL and seq % FFN_TM == 0 and seq % FX_TQ == 0 and seq % ML_CHUNK == 0
    depth = ada_w.shape[0]
    assert depth == 2 and batch <= SUBLANES

    mod = _modulation(c, ada_w, ada_b)
    w_in_b = ffn_w_in.astype(BF16)
    w_out_b = ffn_w_out.astype(BF16)
    x2 = x.reshape(batch * seq, d)

    x2 = _ffn(x2, mod, 0, 0, batch, norm_g[0, 0], w_in_b[0, 0], w_out_b[0, 0])

    H, dk, dv = ML_HEADS, ML_DQK, ML_DV
    w = ml_w_in[0]
    n_qk = H * dk
    w_a = _pair_heads(w[:, :n_qk], w[:, n_qk:2 * n_qk], H, dk)
    w_v = w[:, 2 * n_qk:2 * n_qk + H * dv]
    w_o = w[:, 2 * n_qk + H * dv:2 * n_qk + 2 * H * dv]
    w_g = w[:, 2 * n_qk + 2 * H * dv:]
    w_main = jnp.concatenate([w_a, w_v, w_o], axis=1).astype(BF16)
    cw = ml_conv_w[0]
    conv_w = _pair_heads(cw[:, :n_qk], cw[:, n_qk:], H, dk)
    bias = jnp.concatenate([ml_b_i[0], ml_b_f[0]])
    bias_col = _pad_lanes(bias.reshape(1, 2 * H))
    bias_row = bias.reshape(2 * H, 1)
    a_pre, v, o, gates, gates_t = _proj(
        x2, mod, 0, batch, norm_g[0, 1], w_main, _pad_lanes(w_g).astype(BF16),
        w_g.T.astype(BF16), (2 * n_qk, H * dv, H * dv), F32)
    hs = _mlstm(a_pre, v, o, gates, gates_t, conv_w, bias_col, bias_row, ml_hnorm_g[0], batch)
    x2 = _outproj(hs, ml_w_out[0].astype(BF16), x2, mod, 0, batch)

    x2 = _ffn(x2, mod, 0, 2, batch, norm_g[0, 2], w_in_b[0, 1], w_out_b[0, 1])

    x2 = _ffn(x2, mod, 1, 0, batch, norm_g[1, 0], w_in_b[1, 0], w_out_b[1, 0])

    H, dh = FX_HEADS, FX_DH
    w = fx_w_in[0]
    n = H * dh
    w_a = _pair_heads(w[:, :n], w[:, n:2 * n], H, dh)
    w_main = jnp.concatenate([w_a, w[:, 3 * n:4 * n]], axis=1).astype(BF16)
    w_f = w[:, 4 * n:]
    a, og, fgate, vt = _proj(
        x2, mod, 1, batch, norm_g[1, 1], w_main, _pad_lanes(w_f).astype(BF16),
        w[:, 2 * n:3 * n].T.astype(BF16), (2 * n, n), BF16)
    gqk = jnp.concatenate(
        [fx_qnorm_g[0] * (dh ** -0.5 * LOG2E), fx_knorm_g[0]]).reshape(1, LANES)
    bf_row = _pad_lanes(fx_b_f[0].reshape(1, H))
    q_aug, k_aug = _fox_prep(a, fgate, bf_row, gqk, batch)
    attn = _fox_attn(q_aug, k_aug, vt, og, batch)
    x2 = _outproj(attn, fx_w_out[0].astype(BF16), x2, mod, 1, batch)

    x2 = _ffn(x2, mod, 1, 2, batch, norm_g[1, 2], w_in_b[1, 1], w_out_b[1, 1], final_g=final_g)
    return x2.reshape(batch, seq, d)
```

```python
import functools

import jax
import jax.numpy as jnp
from jax import lax
from jax.experimental import pallas as pl
from jax.experimental.pallas import tpu as pltpu

D_MODEL = 1024
D_FF = 4 * D_MODEL
EPS = 1e-6
FFN_RES = 0.5
CONV_W = 4
ML_HEADS = 8
ML_DV = 128
ML_DQK = 64
FX_HEADS = 16
FX_DH = 64
NEG = -1e30

LANES = 128
SUBLANES = 8
VMEM_LIMIT_BYTES = 56 * 1024 * 1024

FFN_TM = 1024
FFN_TF = 512
PROJ_TM = 512
PROJ_TN = 512
ML_CHUNK = 256
ML_ONES_ROWS = 16
FX_TS = 512
FX_TQ = 512
FX_TK = 512
FX_BIAS_LANE = FX_DH
FX_ONES_ROWS = 16
LOG2E = 1.4426950408889634

BF16 = jnp.bfloat16
F32 = jnp.float32


def _cparams(sem):
    return pltpu.CompilerParams(dimension_semantics=sem, vmem_limit_bytes=VMEM_LIMIT_BYTES)


def _dot(a, b):
    return jnp.dot(a, b, preferred_element_type=F32)


def _dot_nt(a, b):
    return lax.dot_general(a, b, (((1,), (1,)), ((), ())), preferred_element_type=F32)


def _dot_tn(a, b):
    return lax.dot_general(a, b, (((0,), (0,)), ((), ())), preferred_element_type=F32)


def _sigmoid(x):
    return 1.0 / (1.0 + jnp.exp(-x))


def _log_sigmoid(x):
    return jnp.minimum(x, 0.0) - jnp.log(1.0 + jnp.exp(-jnp.abs(x)))


def _split3(x):
    hi = x.astype(BF16)
    r1 = x - hi.astype(F32)
    mid = r1.astype(BF16)
    lo = (r1 - mid.astype(F32)).astype(BF16)
    return hi, mid, lo


def _adaln(x, g, shift, scale):
    ms = jnp.mean(x * x, axis=-1, keepdims=True)
    return (x * lax.rsqrt(ms + EPS) * g) * (1.0 + scale) + shift


def _mod_kernel(c_ref, w_ref, b_ref, o_ref):
    c = c_ref[...]
    cond = (c * _sigmoid(c)).astype(BF16)
    o_ref[0] = _dot(cond, w_ref[0].astype(BF16)) + b_ref[0]


def _modulation(c, ada_w, ada_b):
    depth, d, n = ada_w.shape
    b = c.shape[0]
    c_pad = jnp.zeros((SUBLANES, d), F32).at[:b].set(c)
    tn = 1024
    out = pl.pallas_call(
        _mod_kernel,
        grid=(depth, n // tn),
        in_specs=[
            pl.BlockSpec((SUBLANES, d), lambda l, j: (0, 0)),
            pl.BlockSpec((1, d, tn), lambda l, j: (l, 0, j)),
            pl.BlockSpec((1, 1, tn), lambda l, j: (l, 0, j)),
        ],
        out_specs=pl.BlockSpec((1, SUBLANES, tn), lambda l, j: (l, 0, j)),
        out_shape=jax.ShapeDtypeStruct((depth, SUBLANES, n), F32),
        compiler_params=_cparams(("arbitrary", "arbitrary")),
        name="adaln_mod",
    )(c_pad, ada_w, ada_b.reshape(depth, 1, n))
    mod = out[:, :b].reshape(depth, b, 9, d).transpose(0, 2, 1, 3)
    return mod.reshape(depth * 9 * b, 1, d)


def _mod_spec(layer, idx, batch, rows_per_batch_tiles):
    base = (layer * 9 + idx) * batch
    return pl.BlockSpec((None, 1, D_MODEL),
                        lambda m, *_: (base + m // rows_per_batch_tiles, 0, 0))


def _ffn_kernel(*refs, final_norm, mixer):
    refs = list(refs)
    x_ref, sh_ref, sc_ref, gt_ref, g_ref, wg_ref, wu_ref, wo_ref = refs[:8]
    del refs[:8]
    fg_ref = refs.pop(0) if final_norm else None
    if mixer is not None:
        a_ref, wmix_ref, gmix_ref = refs[:3]
        del refs[:3]
        o_ref, h_scr, acc_scr, x1_scr = refs
    else:
        o_ref, h_scr, acc_scr = refs
    f = pl.program_id(1)

    @pl.when(f == 0)
    def _():
        x1 = x_ref[...]
        if mixer == "rows":
            y = _dot(a_ref[...], wmix_ref[...])
        elif mixer == "time_minor":
            y = jnp.concatenate(
                [_dot_tn(a_ref[j], wmix_ref[...]) for j in range(a_ref.shape[0])], axis=0)
        if mixer is not None:
            x1 = x1 + (1.0 + gmix_ref[...]) * y
            x1_scr[...] = x1
        h = _adaln(x1, g_ref[...], sh_ref[...], sc_ref[...])
        h_scr[...] = h.astype(BF16)
        acc_scr[...] = jnp.zeros_like(acc_scr)

    h = h_scr[...]
    gate = _dot(h, wg_ref[...])
    up = _dot(h, wu_ref[...])
    a = (gate * _sigmoid(gate) * up).astype(BF16)
    acc_scr[...] += _dot(a, wo_ref[...])

    @pl.when(f == pl.num_programs(1) - 1)
    def _():
        x1 = x_ref[...] if mixer is None else x1_scr[...]
        out = x1 + (FFN_RES * (1.0 + gt_ref[...])) * acc_scr[...]
        if final_norm:
            ms = jnp.mean(out * out, axis=-1, keepdims=True)
            out = out * lax.rsqrt(ms + EPS) * fg_ref[...]
        o_ref[...] = out


def _ffn(x2, mod, layer, sub, batch, g, w_in, w_out, final_g=None, mixer_out=None, w_mix=None):
    m, d = x2.shape
    f_dim = w_out.shape[0]
    tm, tf = FFN_TM, FFN_TF
    nf = f_dim // tf
    tiles_per_batch = (m // batch) // tm
    final_norm = final_g is not None
    row = lambda m_, f_: (m_, 0)
    const = lambda m_, f_: (0, 0)
    in_specs = [
        pl.BlockSpec((tm, d), row),
        _mod_spec(layer, sub * 3 + 0, batch, tiles_per_batch),
        _mod_spec(layer, sub * 3 + 1, batch, tiles_per_batch),
        _mod_spec(layer, sub * 3 + 2, batch, tiles_per_batch),
        pl.BlockSpec((1, d), const),
        pl.BlockSpec((d, tf), lambda m_, f_: (0, f_)),
        pl.BlockSpec((d, tf), lambda m_, f_: (0, nf + f_)),
        pl.BlockSpec((tf, d), lambda m_, f_: (f_, 0)),
    ]
    args = [x2, mod, mod, mod, g.reshape(1, d), w_in, w_in, w_out]
    scratch = [pltpu.VMEM((tm, d), BF16), pltpu.VMEM((tm, d), F32)]
    if final_norm:
        in_specs.append(pl.BlockSpec((1, d), const))
        args.append(final_g.reshape(1, d))
    mixer = None
    if mixer_out is not None:
        if mixer_out.ndim == 3:
            mixer = "time_minor"
            sub_tiles = tm // PROJ_TM
            in_specs.append(pl.BlockSpec((sub_tiles, mixer_out.shape[1], PROJ_TM),
                                         lambda m_, f_: (m_, 0, 0)))
        else:
            mixer = "rows"
            in_specs.append(pl.BlockSpec((tm, mixer_out.shape[1]), row))
        in_specs += [pl.BlockSpec(w_mix.shape, const),
                     _mod_spec(layer, 5, batch, tiles_per_batch)]
        args += [mixer_out, w_mix, mod]
        scratch.append(pltpu.VMEM((tm, d), F32))
    return pl.pallas_call(
        functools.partial(_ffn_kernel, final_norm=final_norm, mixer=mixer),
        grid=(m // tm, nf),
        in_specs=in_specs,
        out_specs=pl.BlockSpec((tm, d), row),
        out_shape=jax.ShapeDtypeStruct((m, d), F32),
        scratch_shapes=scratch,
        compiler_params=_cparams(("arbitrary", "arbitrary")),
        name="ffn_final" if final_norm else ("ffn" if mixer is None else "ffn_mix"),
    )(*args)


def _proj_kernel(x_ref, sh_ref, sc_ref, g_ref, w_ref, *refs, widths, n_t):
    h = _adaln(x_ref[...], g_ref[...], sh_ref[...], sc_ref[...]).astype(BF16)
    wt_refs = refs[:n_t]
    row_refs = refs[n_t:n_t + len(widths)]
    t_refs = refs[n_t + len(widths):]
    col = 0
    for o_ref, width in zip(row_refs, widths):
        step = min(width, PROJ_TN)
        for c0 in range(0, width, step):
            o_ref[:, c0:c0 + step] = _dot(
                h, w_ref[:, col + c0:col + c0 + step]).astype(o_ref.dtype)
        col += width
    for wt_ref, t_ref in zip(wt_refs, t_refs):
        rows = wt_ref.shape[0]
        step = min(rows, PROJ_TN)
        for r0 in range(0, rows, step):
            t_ref[0, r0:r0 + step, :] = _dot_nt(wt_ref[r0:r0 + step, :], h).astype(t_ref.dtype)


def _proj(x2, mod, layer, batch, g, w_main, widths, row_dtypes, w_ts, t_dtypes):
    m, d = x2.shape
    tm = PROJ_TM
    tiles_per_batch = (m // batch) // tm
    row = lambda m_: (m_, 0)
    const = lambda m_: (0, 0)
    out_shape = [jax.ShapeDtypeStruct((m, w), dt) for w, dt in zip(widths, row_dtypes)]
    out_specs = [pl.BlockSpec((tm, w), row) for w in widths]
    for w_t, dt in zip(w_ts, t_dtypes):
        out_shape.append(jax.ShapeDtypeStruct((m // tm, w_t.shape[0], tm), dt))
        out_specs.append(pl.BlockSpec((1, w_t.shape[0], tm), lambda m_: (m_, 0, 0)))
    return pl.pallas_call(
        functools.partial(_proj_kernel, widths=tuple(widths), n_t=len(w_ts)),
        grid=(m // tm,),
        in_specs=[
            pl.BlockSpec((tm, d), row),
            _mod_spec(layer, 3, batch, tiles_per_batch),
            _mod_spec(layer, 4, batch, tiles_per_batch),
            pl.BlockSpec((1, d), const),
            pl.BlockSpec(w_main.shape, const),
        ] + [pl.BlockSpec(w_t.shape, const) for w_t in w_ts],
        out_specs=out_specs,
        out_shape=out_shape,
        compiler_params=_cparams(("arbitrary",)),
        name="mixer_proj",
    )(x2, mod, mod, g.reshape(1, d), w_main, *w_ts)


def _mlstm_kernel(a_ref, vo_ref, gt_ref, cw_ref, bias_ref, hg_ref, out_ref, xbuf, c_scr, m_scr):
    L = ML_CHUNK
    H = ML_HEADS
    d = a_ref.shape[1]
    c = pl.program_id(1)

    @pl.when(c == 0)
    def _():
        xbuf[0:SUBLANES, :] = jnp.zeros((SUBLANES, xbuf.shape[1]), F32)
        c_scr[...] = jnp.zeros_like(c_scr)
        m_scr[...] = jnp.zeros_like(m_scr)

    xbuf[SUBLANES:SUBLANES + L, :] = a_ref[...].astype(F32)
    y = cw_ref[CONV_W - 1:CONV_W, :] * xbuf[SUBLANES:SUBLANES + L, :]
    for j in range(CONV_W - 1):
        off = SUBLANES - (CONV_W - 1) + j
        y = y + cw_ref[j:j + 1, :] * xbuf[off:off + L, :]
    xbuf[0:SUBLANES, :] = xbuf[L:L + SUBLANES, :]
    act = y * _sigmoid(y)

    gt = gt_ref[0] + bias_ref[...]
    lf_all = _log_sigmoid(gt) * LOG2E
    s_i = lax.broadcasted_iota(jnp.int32, (L, L), 0)
    t_i = lax.broadcasted_iota(jnp.int32, (L, L), 1)
    causal_t = s_i <= t_i
    tup = jnp.where(causal_t, 1.0, 0.0).astype(BF16)
    r_hi, r_mid, r_lo = _split3(lf_all)
    b = (_dot(r_hi, tup) + _dot(r_mid, tup) + _dot(r_lo, tup))[H:2 * H]
    i2 = gt[0:H] * LOG2E
    rowv = i2 - b
    lane = lax.broadcasted_iota(jnp.int32, (H, L), 1)
    cmax = rowv
    shift = 1
    while shift < L:
        cmax = jnp.maximum(cmax, jnp.where(lane >= shift, pltpu.roll(cmax, shift, 1), NEG))
        shift *= 2
    m_prev = m_scr[...]
    u = jnp.maximum(m_prev, cmax)
    w_inter = jnp.exp2(m_prev - u)
    inv_floor = jnp.exp2(-(b + u))
    g_tot = jnp.broadcast_to(b[:, L - 1:L], (H, L))
    a_w = g_tot - b + i2
    m_new = jnp.maximum(g_tot + m_prev, jnp.max(a_w, axis=1, keepdims=True))
    wk = jnp.exp2(a_w - m_new)
    decay = jnp.exp2(g_tot + m_prev - m_new)
    m_scr[...] = m_new

    rv_parts = [p.astype(F32) for p in _split3(rowv)]
    nu_parts = [p.astype(F32) for p in _split3(-u)]
    sub16 = lax.broadcasted_iota(jnp.int32, (2 * SUBLANES, L), 0)
    row16 = lambda x, h_: jnp.broadcast_to(x[h_:h_ + 1, :], (2 * SUBLANES, L))

    lane128 = lax.broadcasted_iota(jnp.int32, (L, LANES), 1)
    first_half = lane128 < ML_DQK
    ones_rows = jnp.ones((ML_ONES_ROWS, L), BF16)

    for h in range(H):
        a_h = act[:, h * LANES:(h + 1) * LANES]
        qb = (jnp.where(first_half, a_h, 0.0) * (ML_DQK ** -0.5)).astype(BF16)
        kb = jnp.where(first_half, pltpu.roll(a_h, ML_DQK, 1), 0.0).astype(BF16)

        src = jnp.where(sub16 == 0, row16(rv_parts[0], h), jnp.where(
            sub16 == 1, row16(rv_parts[1], h), jnp.where(
                sub16 == 2, row16(rv_parts[2], h), jnp.where(sub16 < 6, 1.0, 0.0))))
        dst = jnp.where(sub16 < 3, 1.0, jnp.where(
            sub16 == 3, row16(nu_parts[0], h), jnp.where(
                sub16 == 4, row16(nu_parts[1], h), jnp.where(
                    sub16 == 5, row16(nu_parts[2], h), 0.0))))
        z = _dot_tn(src.astype(BF16), dst.astype(BF16))
        w_intra = jnp.exp2(jnp.where(causal_t, z, NEG))
        s_t = (_dot_nt(kb, qb) * w_intra).astype(BF16)

        va = jnp.concatenate([vo_ref[0, h * ML_DV:(h + 1) * ML_DV, :], ones_rows], axis=0)
        c_prev = c_scr[h]
        num = _dot(va, s_t) + _dot_nt(c_prev.astype(BF16), qb) * w_inter[h:h + 1, :]
        den = num[ML_DV:ML_DV + 1, :]
        h_t = num[:ML_DV, :] / jnp.maximum(jnp.abs(den), inv_floor[h:h + 1, :])

        hn = h_t * lax.rsqrt(jnp.mean(h_t * h_t, axis=0, keepdims=True) + EPS)
        hn = hn * hg_ref[h * ML_DV:(h + 1) * ML_DV, :]
        og = vo_ref[0, d + h * ML_DV:d + (h + 1) * ML_DV, :].astype(F32)
        out_ref[0, h * ML_DV:(h + 1) * ML_DV, :] = (_sigmoid(og) * hn).astype(out_ref.dtype)

        vw = (va.astype(F32) * wk[h:h + 1, :]).astype(BF16)
        c_scr[h] = decay[h:h + 1, 0:LANES] * c_prev + _dot(vw, kb)


def _mlstm(a_pre, vo_t, gates_t, conv_w, bias, hnorm_g, batch):
    m, d = a_pre.shape
    seq = m // batch
    L = ML_CHUNK
    nc = seq // L
    sub = PROJ_TM // L
    const = lambda b, c: (0, 0)
    tmin = lambda b, c: ((b * nc + c) // sub, 0, (b * nc + c) % sub)
    hg = jnp.broadcast_to(hnorm_g.reshape(d, 1), (d, L))
    bias_b = jnp.broadcast_to(bias.reshape(2 * ML_HEADS, 1), (2 * ML_HEADS, L))
    return pl.pallas_call(
        _mlstm_kernel,
        grid=(batch, nc),
        in_specs=[
            pl.BlockSpec((L, d), lambda b, c: (b * nc + c, 0)),
            pl.BlockSpec((1, 2 * d, L), tmin),
            pl.BlockSpec((1, 2 * ML_HEADS, L), tmin),
            pl.BlockSpec(conv_w.shape, const),
            pl.BlockSpec(bias_b.shape, const),
            pl.BlockSpec(hg.shape, const),
        ],
        out_specs=pl.BlockSpec((1, d, L), tmin),
        out_shape=jax.ShapeDtypeStruct((m // PROJ_TM, d, PROJ_TM), BF16),
        scratch_shapes=[
            pltpu.VMEM((SUBLANES + L, d), F32),
            pltpu.VMEM((ML_HEADS, ML_DV + ML_ONES_ROWS, LANES), F32),
            pltpu.VMEM((ML_HEADS, L), F32),
        ],
        compiler_params=_cparams(("arbitrary", "arbitrary")),
        name="mlstm_scan",
    )(a_pre, vo_t, gates_t, conv_w, bias_b, hg)


def _fox_prep_kernel(a_ref, f_ref, bf_ref, gqk_ref, place_ref, q_ref, k_ref, carry):
    ts = FX_TS
    H = FX_HEADS
    t = pl.program_id(1)

    @pl.when(t == 0)
    def _():
        carry[...] = jnp.zeros_like(carry)

    logf = _log_sigmoid(f_ref[...] + bf_ref[...]) * LOG2E
    r_i = lax.broadcasted_iota(jnp.int32, (ts, ts), 0)
    c_i = lax.broadcasted_iota(jnp.int32, (ts, ts), 1)
    tlow = jnp.where(r_i >= c_i, 1.0, 0.0).astype(BF16)
    hi, mid, lo = _split3(logf)
    fcum = _dot(tlow, hi) + _dot(tlow, mid) + _dot(tlow, lo) + carry[0:1, :]
    carry[...] = jnp.broadcast_to(fcum[ts - 1:ts, :], carry.shape)

    lane = lax.broadcasted_iota(jnp.int32, (ts, LANES), 1)
    f_m = jnp.where(lane < H, fcum, 0.0)
    f_hi = f_m.astype(BF16).astype(F32)
    f_r1 = f_m - f_hi
    f_mid = f_r1.astype(BF16).astype(F32)
    f_lo = f_r1 - f_mid
    f_parts = (f_hi + pltpu.roll(f_mid, H, 1) + pltpu.roll(f_lo, 2 * H, 1)
               + jnp.where(lane == 3 * H, 1.0, 0.0)).astype(BF16)

    first_half = lane < FX_DH
    gqk = gqk_ref[...]
    for h in range(H):
        a = a_ref[:, h * LANES:(h + 1) * LANES].astype(F32)
        lhs = jnp.concatenate([(a * a).astype(BF16), f_parts], axis=1)
        res = _dot(lhs, place_ref[h])
        rinv = lax.rsqrt(res[:, :LANES] * (1.0 / FX_DH) + EPS)
        bias = res[:, LANES:]
        a_n = a * rinv * gqk
        q_ref[:, h * LANES:(h + 1) * LANES] = jnp.where(first_half, a_n, bias).astype(BF16)
        k_ref[:, h * LANES:(h + 1) * LANES] = pltpu.roll(
            jnp.where(first_half, bias, a_n), FX_DH, 1).astype(BF16)


def _fox_placement():
    H = FX_HEADS
    r = lax.broadcasted_iota(jnp.int32, (H, 2 * LANES, 2 * LANES), 1)
    c = lax.broadcasted_iota(jnp.int32, (H, 2 * LANES, 2 * LANES), 2)
    h = lax.broadcasted_iota(jnp.int32, (H, 2 * LANES, 2 * LANES), 0)
    half = (r < LANES) & (c < LANES) & ((r < FX_DH) == (c < FX_DH))
    src = r - LANES
    dst = c - LANES
    qb = FX_BIAS_LANE
    one_src = src == 3 * H
    plus = ((src == h) & (dst == qb)) | ((src == H + h) & (dst == qb + 1)) | (
        (src == 2 * H + h) & (dst == qb + 2)) | (one_src & (dst >= qb + 3) & (dst < qb + 6)) | (
        one_src & (dst >= 0) & (dst < 3))
    minus = ((src == h) & (dst == 3)) | ((src == H + h) & (dst == 4)) | (
        (src == 2 * H + h) & (dst == 5))
    in_place = (r >= LANES) & (c >= LANES)
    val = jnp.where(half | (in_place & plus), 1.0, jnp.where(in_place & minus, -1.0, 0.0))
    return val.astype(BF16)


def _fox_prep(a, fgate, bf_row, gqk, batch):
    m = a.shape[0]
    seq = m // batch
    ts = FX_TS
    nt = seq // ts
    wide = FX_HEADS * LANES
    row = lambda b, t: (b * nt + t, 0)
    const = lambda b, t: (0, 0)
    return pl.pallas_call(
        _fox_prep_kernel,
        grid=(batch, nt),
        in_specs=[
            pl.BlockSpec((ts, wide), row),
            pl.BlockSpec((ts, LANES), row),
            pl.BlockSpec((1, LANES), const),
            pl.BlockSpec((1, LANES), const),
            pl.BlockSpec((FX_HEADS, 2 * LANES, 2 * LANES), lambda b, t: (0, 0, 0)),
        ],
        out_specs=[pl.BlockSpec((ts, wide), row)] * 2,
        out_shape=[jax.ShapeDtypeStruct((m, wide), BF16)] * 2,
        scratch_shapes=[pltpu.VMEM((SUBLANES, LANES), F32)],
        compiler_params=_cparams(("arbitrary", "arbitrary")),
        name="fox_prep",
    )(a, fgate, bf_row, gqk, _fox_placement())


def _fox_attn_kernel(qtab_ref, ktab_ref, q_ref, k_ref, vt_ref, og_ref, o_ref, m_st, acc_st,
                     sa_scr, mxa_scr, sb_scr, mxb_scr, *, n_diag, n_blocks):
    tq, tk = FX_TQ, FX_TK
    key_i = lax.broadcasted_iota(jnp.int32, (tk, tq), 0)
    qry_i = lax.broadcasted_iota(jnp.int32, (tk, tq), 1)
    causal = key_i <= qry_i
    ones_rows = jnp.ones((FX_ONES_ROWS, tk), BF16)

    def score(e, s_ref, mx_ref):
        q0 = pl.multiple_of(qtab_ref[e] * tq, tq)
        k0 = pl.multiple_of(ktab_ref[e] * tk, tk)
        for hh in range(2):
            q = q_ref[0, pl.ds(q0, tq), hh * LANES:(hh + 1) * LANES]
            k = k_ref[0, pl.ds(k0, tk), hh * LANES:(hh + 1) * LANES]
            s = _dot_nt(k, q)
            s_ref[hh] = s
            mx_ref[hh] = jnp.max(s, axis=0, keepdims=True)

    def values(kj, hh):
        return jnp.concatenate(
            [vt_ref[0, kj, hh * FX_DH:(hh + 1) * FX_DH, :], ones_rows], axis=0)

    def consume_diag(e, s_ref):
        qi = qtab_ref[e]
        for hh in range(2):
            s = jnp.where(causal, s_ref[hh], NEG)
            m_new = jnp.max(s, axis=0, keepdims=True)
            p = jnp.exp2(s - m_new).astype(BF16)
            acc_st[qi, hh] = _dot(values(qi, hh), p)
            m_st[qi, hh] = m_new

    def consume(e, s_ref, mx_ref):
        qi = qtab_ref[e]
        kj = ktab_ref[e]
        for hh in range(2):
            m_prev = m_st[qi, hh]
            m_new = jnp.maximum(m_prev, mx_ref[hh])
            p = jnp.exp2(s_ref[hh] - m_new).astype(BF16)
            corr = jnp.exp2(m_prev - m_new)
            acc_st[qi, hh] = corr * acc_st[qi, hh] + _dot(values(kj, hh), p)
            m_st[qi, hh] = m_new

    def diag_body(i, carry):
        e = 2 * i
        score(e + 1, sb_scr, mxb_scr)
        consume_diag(e, sa_scr)
        score(e + 2, sa_scr, mxa_scr)
        consume_diag(e + 1, sb_scr)
        return carry

    def body(i, carry):
        e = 2 * i
        score(e + 1, sb_scr, mxb_scr)
        consume(e, sa_scr, mxa_scr)
        score(e + 2, sa_scr, mxa_scr)
        consume(e + 1, sb_scr, mxb_scr)
        return carry

    score(0, sa_scr, mxa_scr)
    lax.fori_loop(0, n_diag // 2, diag_body, 0)
    lax.fori_loop(n_diag // 2, n_blocks // 2, body, 0)

    def finalize(qi, carry):
        q0 = pl.multiple_of(qi * tq, tq)
        outs = []
        for hh in range(2):
            acc = acc_st[qi, hh]
            outs.append(acc[:FX_DH, :] / acc[FX_DH:FX_DH + 1, :])
        out = jnp.concatenate(outs, axis=0).T
        og = og_ref[0, pl.ds(q0, tq), :].astype(F32)
        o_ref[0, pl.ds(q0, tq), :] = (_sigmoid(og) * out).astype(o_ref.dtype)
        return carry

    lax.fori_loop(0, n_diag, finalize, 0)


def _fox_block_tables(nq):
    qs = list(range(nq)) + [qi for qi in range(nq) for _ in range(qi)] + [0]
    ks = list(range(nq)) + [kj for qi in range(nq) for kj in range(qi)] + [0]
    return jnp.asarray(qs, jnp.int32), jnp.asarray(ks, jnp.int32)


def _fox_attn(q_aug, k_aug, vt, og, batch):
    m, wide = q_aug.shape
    seq = m // batch
    tq, tk = FX_TQ, FX_TK
    assert tq == tk == PROJ_TM
    nq = seq // tq
    n_blocks = nq * (nq + 1) // 2
    assert nq % 2 == 0 and n_blocks % 2 == 0
    pairs = FX_HEADS // 2
    q3 = q_aug.reshape(batch, seq, wide)
    k3 = k_aug.reshape(batch, seq, wide)
    vt4 = vt.reshape(batch, seq // tk, FX_HEADS * FX_DH, tk)
    og3 = og.reshape(batch, seq, FX_HEADS * FX_DH)
    qtab, ktab = _fox_block_tables(nq)
    rows = FX_DH + FX_ONES_ROWS
    grid_spec = pltpu.PrefetchScalarGridSpec(
        num_scalar_prefetch=2,
        grid=(batch, pairs),
        in_specs=[
            pl.BlockSpec((1, seq, 2 * LANES), lambda b, p, *_: (b, 0, p)),
            pl.BlockSpec((1, seq, 2 * LANES), lambda b, p, *_: (b, 0, p)),
            pl.BlockSpec((1, seq // tk, 2 * FX_DH, tk), lambda b, p, *_: (b, 0, p, 0)),
            pl.BlockSpec((1, seq, LANES), lambda b, p, *_: (b, 0, p)),
        ],
        out_specs=pl.BlockSpec((1, seq, LANES), lambda b, p, *_: (b, 0, p)),
        scratch_shapes=[pltpu.VMEM((nq, 2, 1, tq), F32),
                        pltpu.VMEM((nq, 2, rows, tq), F32),
                        pltpu.VMEM((2, tk, tq), F32), pltpu.VMEM((2, 1, tq), F32),
                        pltpu.VMEM((2, tk, tq), F32), pltpu.VMEM((2, 1, tq), F32)],
    )
    out = pl.pallas_call(
        functools.partial(_fox_attn_kernel, n_diag=nq, n_blocks=n_blocks),
        grid_spec=grid_spec,
        out_shape=jax.ShapeDtypeStruct((batch, seq, FX_HEADS * FX_DH), BF16),
        compiler_params=_cparams(("arbitrary", "arbitrary")),
        name="fox_attn",
    )(qtab, ktab, q3, k3, vt4, og3)
    return out.reshape(m, FX_HEADS * FX_DH)


def _pair_heads(wq, wk, heads, dh):
    lead = wq.shape[0]
    return jnp.concatenate(
        [wq.reshape(lead, heads, dh), wk.reshape(lead, heads, dh)], axis=2
    ).reshape(lead, heads * 2 * dh)


def _pad_lanes(w):
    return jnp.zeros((w.shape[0], LANES), w.dtype).at[:, :w.shape[1]].set(w)


def kernel(x, c, ada_w, ada_b, norm_g, ffn_w_in, ffn_w_out, ml_w_in, ml_conv_w, ml_b_i, ml_b_f,
           ml_hnorm_g, ml_w_out, fx_w_in, fx_b_f, fx_qnorm_g, fx_knorm_g, fx_w_out, final_g):
    batch, seq, d = x.shape
    assert d == D_MODEL and seq % FFN_TM == 0 and seq % FX_TQ == 0 and seq % ML_CHUNK == 0
    depth = ada_w.shape[0]
    assert depth == 2 and batch <= SUBLANES

    mod = _modulation(c, ada_w, ada_b)
    w_in_b = ffn_w_in.astype(BF16)
    w_out_b = ffn_w_out.astype(BF16)
    x2 = x.reshape(batch * seq, d)

    x2 = _ffn(x2, mod, 0, 0, batch, norm_g[0, 0], w_in_b[0, 0], w_out_b[0, 0])

    H, dk, dv = ML_HEADS, ML_DQK, ML_DV
    w = ml_w_in[0]
    n_qk = H * dk
    w_a = _pair_heads(w[:, :n_qk], w[:, n_qk:2 * n_qk], H, dk)
    w_v = w[:, 2 * n_qk:2 * n_qk + H * dv]
    w_o = w[:, 2 * n_qk + H * dv:2 * n_qk + 2 * H * dv]
    w_g = w[:, 2 * n_qk + 2 * H * dv:]
    w_vo_t = jnp.concatenate([w_v, w_o], axis=1).T.astype(BF16)
    cw = ml_conv_w[0]
    conv_w = _pair_heads(cw[:, :n_qk], cw[:, n_qk:], H, dk)
    bias = jnp.concatenate([ml_b_i[0], ml_b_f[0]])
    a_pre, vo_t, gates_t = _proj(
        x2, mod, 0, batch, norm_g[0, 1], w_a.astype(BF16), (2 * n_qk,), (BF16,),
        (w_vo_t, w_g.T.astype(BF16)), (BF16, F32))
    hs_t = _mlstm(a_pre, vo_t, gates_t, conv_w, bias, ml_hnorm_g[0], batch)
    x2 = _ffn(x2, mod, 0, 2, batch, norm_g[0, 2], w_in_b[0, 1], w_out_b[0, 1],
              mixer_out=hs_t, w_mix=ml_w_out[0].astype(BF16))

    x2 = _ffn(x2, mod, 1, 0, batch, norm_g[1, 0], w_in_b[1, 0], w_out_b[1, 0])

    H, dh = FX_HEADS, FX_DH
    w = fx_w_in[0]
    n = H * dh
    w_a = _pair_heads(w[:, :n], w[:, n:2 * n], H, dh)
    w_main = jnp.concatenate([w_a, w[:, 3 * n:4 * n], _pad_lanes(w[:, 4 * n:])], axis=1)
    a, og, fgate, vt = _proj(
        x2, mod, 1, batch, norm_g[1, 1], w_main.astype(BF16), (2 * n, n, LANES),
        (BF16, BF16, F32), (w[:, 2 * n:3 * n].T.astype(BF16),), (BF16,))
    gqk = jnp.concatenate(
        [fx_qnorm_g[0] * (dh ** -0.5 * LOG2E), fx_knorm_g[0]]).reshape(1, LANES)
    bf_row = _pad_lanes(fx_b_f[0].reshape(1, H))
    q_aug, k_aug = _fox_prep(a, fgate, bf_row, gqk, batch)
    attn = _fox_attn(q_aug, k_aug, vt, og, batch)
    x2 = _ffn(x2, mod, 1, 2, batch, norm_g[1, 2], w_in_b[1, 1], w_out_b[1, 1], final_g=final_g,
              mixer_out=attn, w_mix=fx_w_out[0].astype(BF16))
    return x2.reshape(batch, seq, d)
```

```python
import functools

import jax
import jax.numpy as jnp
from jax import lax
from jax.experimental import pallas as pl
from jax.experimental.pallas import tpu as pltpu

D_MODEL = 1024
D_FF = 4 * D_MODEL
EPS = 1e-6
FFN_RES = 0.5
CONV_W = 4
ML_HEADS = 8
ML_DV = 128
ML_DQK = 64
FX_HEADS = 16
FX_DH = 64
NEG = -1e30

LANES = 128
SUBLANES = 8
VMEM_LIMIT_BYTES = 56 * 1024 * 1024

FFN_TM = 1024
FFN_TF = 1024
PROJ_TM = 512
PROJ_TN = 512
ML_CHUNK = 256
ML_ONES_ROWS = 16
FX_TS = 512
FX_TQ = 512
FX_TK = 512
FX_BIAS_LANE = FX_DH
FX_AHEAD = 1
FX_BUFS = FX_AHEAD + 1
FX_UNROLL = 8
FX_ONES_ROWS = 16
LOG2E = 1.4426950408889634

BF16 = jnp.bfloat16
F32 = jnp.float32


def _cparams(sem):
    return pltpu.CompilerParams(dimension_semantics=sem, vmem_limit_bytes=VMEM_LIMIT_BYTES)


def _dot(a, b):
    return jnp.dot(a, b, preferred_element_type=F32)


def _dot_nt(a, b):
    return lax.dot_general(a, b, (((1,), (1,)), ((), ())), preferred_element_type=F32)


def _dot_tn(a, b):
    return lax.dot_general(a, b, (((0,), (0,)), ((), ())), preferred_element_type=F32)


def _sigmoid(x):
    return 1.0 / (1.0 + jnp.exp(-x))


def _log_sigmoid(x):
    return jnp.minimum(x, 0.0) - jnp.log(1.0 + jnp.exp(-jnp.abs(x)))


def _split3(x):
    hi = x.astype(BF16)
    r1 = x - hi.astype(F32)
    mid = r1.astype(BF16)
    lo = (r1 - mid.astype(F32)).astype(BF16)
    return hi, mid, lo


def _adaln(x, g, shift, scale):
    ms = jnp.mean(x * x, axis=-1, keepdims=True)
    return (x * lax.rsqrt(ms + EPS) * g) * (1.0 + scale) + shift


def _mod_kernel(c_ref, w_ref, b_ref, o_ref):
    c = c_ref[...]
    cond = (c * _sigmoid(c)).astype(BF16)
    o_ref[0] = _dot(cond, w_ref[0].astype(BF16)) + b_ref[0]


def _modulation(c, ada_w, ada_b):
    depth, d, n = ada_w.shape
    b = c.shape[0]
    c_pad = jnp.zeros((SUBLANES, d), F32).at[:b].set(c)
    tn = 1024
    out = pl.pallas_call(
        _mod_kernel,
        grid=(depth, n // tn),
        in_specs=[
            pl.BlockSpec((SUBLANES, d), lambda l, j: (0, 0)),
            pl.BlockSpec((1, d, tn), lambda l, j: (l, 0, j)),
            pl.BlockSpec((1, 1, tn), lambda l, j: (l, 0, j)),
        ],
        out_specs=pl.BlockSpec((1, SUBLANES, tn), lambda l, j: (l, 0, j)),
        out_shape=jax.ShapeDtypeStruct((depth, SUBLANES, n), F32),
        compiler_params=_cparams(("arbitrary", "arbitrary")),
        name="adaln_mod",
    )(c_pad, ada_w, ada_b.reshape(depth, 1, n))
    mod = out[:, :b].reshape(depth, b, 9, d).transpose(0, 2, 1, 3)
    return mod.reshape(depth * 9 * b, 1, d)


def _mod_spec(layer, idx, batch, rows_per_batch_tiles):
    base = (layer * 9 + idx) * batch
    return pl.BlockSpec((None, 1, D_MODEL),
                        lambda m, *_: (base + m // rows_per_batch_tiles, 0, 0))


def _ffn_kernel(*refs, final_norm, mixer):
    refs = list(refs)
    x_ref, sh_ref, sc_ref, gt_ref, g_ref, wg_ref, wu_ref, wo_ref = refs[:8]
    del refs[:8]
    fg_ref = refs.pop(0) if final_norm else None
    if mixer is not None:
        a_ref, wmix_ref, gmix_ref = refs[:3]
        del refs[:3]
        o_ref, h_scr, acc_scr, x1_scr = refs
    else:
        o_ref, h_scr, acc_scr = refs
    f = pl.program_id(1)

    @pl.when(f == 0)
    def _():
        x1 = x_ref[...]
        if mixer == "rows":
            y = _dot(a_ref[...], wmix_ref[...])
        elif mixer == "time_minor":
            y = jnp.concatenate(
                [_dot_tn(a_ref[j], wmix_ref[...]) for j in range(a_ref.shape[0])], axis=0)
        if mixer is not None:
            x1 = x1 + (1.0 + gmix_ref[...]) * y
            x1_scr[...] = x1
        h = _adaln(x1, g_ref[...], sh_ref[...], sc_ref[...])
        h_scr[...] = h.astype(BF16)
        acc_scr[...] = jnp.zeros_like(acc_scr)

    h = h_scr[...]
    gate = _dot(h, wg_ref[...])
    up = _dot(h, wu_ref[...])
    a = (gate * _sigmoid(gate) * up).astype(BF16)
    acc_scr[...] += _dot(a, wo_ref[...])

    @pl.when(f == pl.num_programs(1) - 1)
    def _():
        x1 = x_ref[...] if mixer is None else x1_scr[...]
        out = x1 + (FFN_RES * (1.0 + gt_ref[...])) * acc_scr[...]
        if final_norm:
            ms = jnp.mean(out * out, axis=-1, keepdims=True)
            out = out * lax.rsqrt(ms + EPS) * fg_ref[...]
        o_ref[...] = out


def _ffn(x2, mod, layer, sub, batch, g, w_in, w_out, final_g=None, mixer_out=None, w_mix=None):
    m, d = x2.shape
    f_dim = w_out.shape[0]
    tm, tf = FFN_TM, FFN_TF
    nf = f_dim // tf
    tiles_per_batch = (m // batch) // tm
    final_norm = final_g is not None
    row = lambda m_, f_: (m_, 0)
    const = lambda m_, f_: (0, 0)
    in_specs = [
        pl.BlockSpec((tm, d), row),
        _mod_spec(layer, sub * 3 + 0, batch, tiles_per_batch),
        _mod_spec(layer, sub * 3 + 1, batch, tiles_per_batch),
        _mod_spec(layer, sub * 3 + 2, batch, tiles_per_batch),
        pl.BlockSpec((1, d), const),
        pl.BlockSpec((d, tf), lambda m_, f_: (0, f_)),
        pl.BlockSpec((d, tf), lambda m_, f_: (0, nf + f_)),
        pl.BlockSpec((tf, d), lambda m_, f_: (f_, 0)),
    ]
    args = [x2, mod, mod, mod, g.reshape(1, d), w_in, w_in, w_out]
    scratch = [pltpu.VMEM((tm, d), BF16), pltpu.VMEM((tm, d), F32)]
    if final_norm:
        in_specs.append(pl.BlockSpec((1, d), const))
        args.append(final_g.reshape(1, d))
    mixer = None
    if mixer_out is not None:
        if mixer_out.ndim == 3:
            mixer = "time_minor"
            sub_tiles = tm // PROJ_TM
            in_specs.append(pl.BlockSpec((sub_tiles, mixer_out.shape[1], PROJ_TM),
                                         lambda m_, f_: (m_, 0, 0)))
        else:
            mixer = "rows"
            in_specs.append(pl.BlockSpec((tm, mixer_out.shape[1]), row))
        in_specs += [pl.BlockSpec(w_mix.shape, const),
                     _mod_spec(layer, 5, batch, tiles_per_batch)]
        args += [mixer_out, w_mix, mod]
        scratch.append(pltpu.VMEM((tm, d), F32))
    return pl.pallas_call(
        functools.partial(_ffn_kernel, final_norm=final_norm, mixer=mixer),
        grid=(m // tm, nf),
        in_specs=in_specs,
        out_specs=pl.BlockSpec((tm, d), row),
        out_shape=jax.ShapeDtypeStruct((m, d), F32),
        scratch_shapes=scratch,
        compiler_params=_cparams(("arbitrary", "arbitrary")),
        name="ffn_final" if final_norm else ("ffn" if mixer is None else "ffn_mix"),
    )(*args)


def _proj_kernel(x_ref, sh_ref, sc_ref, g_ref, w_ref, *refs, widths, n_t):
    h = _adaln(x_ref[...], g_ref[...], sh_ref[...], sc_ref[...]).astype(BF16)
    wt_refs = refs[:n_t]
    row_refs = refs[n_t:n_t + len(widths)]
    t_refs = refs[n_t + len(widths):]
    col = 0
    for o_ref, width in zip(row_refs, widths):
        step = min(width, PROJ_TN)
        for c0 in range(0, width, step):
            o_ref[:, c0:c0 + step] = _dot(
                h, w_ref[:, col + c0:col + c0 + step]).astype(o_ref.dtype)
        col += width
    for wt_ref, t_ref in zip(wt_refs, t_refs):
        rows = wt_ref.shape[0]
        step = min(rows, PROJ_TN)
        for r0 in range(0, rows, step):
            t_ref[0, r0:r0 + step, :] = _dot_nt(wt_ref[r0:r0 + step, :], h).astype(t_ref.dtype)


def _proj(x2, mod, layer, batch, g, w_main, widths, row_dtypes, w_ts, t_dtypes):
    m, d = x2.shape
    tm = PROJ_TM
    tiles_per_batch = (m // batch) // tm
    row = lambda m_: (m_, 0)
    const = lambda m_: (0, 0)
    out_shape = [jax.ShapeDtypeStruct((m, w), dt) for w, dt in zip(widths, row_dtypes)]
    out_specs = [pl.BlockSpec((tm, w), row) for w in widths]
    for w_t, dt in zip(w_ts, t_dtypes):
        out_shape.append(jax.ShapeDtypeStruct((m // tm, w_t.shape[0], tm), dt))
        out_specs.append(pl.BlockSpec((1, w_t.shape[0], tm), lambda m_: (m_, 0, 0)))
    return pl.pallas_call(
        functools.partial(_proj_kernel, widths=tuple(widths), n_t=len(w_ts)),
        grid=(m // tm,),
        in_specs=[
            pl.BlockSpec((tm, d), row),
            _mod_spec(layer, 3, batch, tiles_per_batch),
            _mod_spec(layer, 4, batch, tiles_per_batch),
            pl.BlockSpec((1, d), const),
            pl.BlockSpec(w_main.shape, const),
        ] + [pl.BlockSpec(w_t.shape, const) for w_t in w_ts],
        out_specs=out_specs,
        out_shape=out_shape,
        compiler_params=_cparams(("arbitrary",)),
        name="mixer_proj",
    )(x2, mod, mod, g.reshape(1, d), w_main, *w_ts)


def _mlstm_kernel(a_ref, vo_ref, gt_ref, cw_ref, bias_ref, hg_ref, out_ref, xbuf, c_scr, m_scr):
    L = ML_CHUNK
    H = ML_HEADS
    d = a_ref.shape[1]
    c = pl.program_id(1)

    @pl.when(c == 0)
    def _():
        xbuf[0:SUBLANES, :] = jnp.zeros((SUBLANES, xbuf.shape[1]), F32)
        c_scr[...] = jnp.zeros_like(c_scr)
        m_scr[...] = jnp.zeros_like(m_scr)

    xbuf[SUBLANES:SUBLANES + L, :] = a_ref[...].astype(F32)
    y = cw_ref[CONV_W - 1:CONV_W, :] * xbuf[SUBLANES:SUBLANES + L, :]
    for j in range(CONV_W - 1):
        off = SUBLANES - (CONV_W - 1) + j
        y = y + cw_ref[j:j + 1, :] * xbuf[off:off + L, :]
    xbuf[0:SUBLANES, :] = xbuf[L:L + SUBLANES, :]
    act = y * _sigmoid(y)

    gt = gt_ref[0] + bias_ref[...]
    lf_all = _log_sigmoid(gt) * LOG2E
    s_i = lax.broadcasted_iota(jnp.int32, (L, L), 0)
    t_i = lax.broadcasted_iota(jnp.int32, (L, L), 1)
    causal_t = s_i <= t_i
    tup = jnp.where(causal_t, 1.0, 0.0).astype(BF16)
    r_hi, r_mid, r_lo = _split3(lf_all)
    b = (_dot(r_hi, tup) + _dot(r_mid, tup) + _dot(r_lo, tup))[H:2 * H]
    i2 = gt[0:H] * LOG2E
    rowv = i2 - b
    lane = lax.broadcasted_iota(jnp.int32, (H, L), 1)
    cmax = rowv
    shift = 1
    while shift < L:
        cmax = jnp.maximum(cmax, jnp.where(lane >= shift, pltpu.roll(cmax, shift, 1), NEG))
        shift *= 2
    m_prev = m_scr[...]
    u = jnp.maximum(m_prev, cmax)
    w_inter = jnp.exp2(m_prev - u)
    inv_floor = jnp.exp2(-(b + u))
    g_tot = jnp.broadcast_to(b[:, L - 1:L], (H, L))
    a_w = g_tot - b + i2
    m_new = jnp.maximum(g_tot + m_prev, jnp.max(a_w, axis=1, keepdims=True))
    wk = jnp.exp2(a_w - m_new)
    decay = jnp.exp2(g_tot + m_prev - m_new)
    m_scr[...] = m_new

    rv_parts = [p.astype(F32) for p in _split3(rowv)]
    nu_parts = [p.astype(F32) for p in _split3(-u)]
    sub16 = lax.broadcasted_iota(jnp.int32, (2 * SUBLANES, L), 0)
    row16 = lambda x, h_: jnp.broadcast_to(x[h_:h_ + 1, :], (2 * SUBLANES, L))

    lane128 = lax.broadcasted_iota(jnp.int32, (L, LANES), 1)
    first_half = lane128 < ML_DQK
    ones_rows = jnp.ones((ML_ONES_ROWS, L), BF16)

    for h in range(H):
        a_h = act[:, h * LANES:(h + 1) * LANES]
        qb = (jnp.where(first_half, a_h, 0.0) * (ML_DQK ** -0.5)).astype(BF16)
        kb = jnp.where(first_half, pltpu.roll(a_h, ML_DQK, 1), 0.0).astype(BF16)

        src = jnp.where(sub16 == 0, row16(rv_parts[0], h), jnp.where(
            sub16 == 1, row16(rv_parts[1], h), jnp.where(
                sub16 == 2, row16(rv_parts[2], h), jnp.where(sub16 < 6, 1.0, 0.0))))
        dst = jnp.where(sub16 < 3, 1.0, jnp.where(
            sub16 == 3, row16(nu_parts[0], h), jnp.where(
                sub16 == 4, row16(nu_parts[1], h), jnp.where(
                    sub16 == 5, row16(nu_parts[2], h), 0.0))))
        z = _dot_tn(src.astype(BF16), dst.astype(BF16))
        w_intra = jnp.exp2(jnp.where(causal_t, z, NEG))
        s_t = (_dot_nt(kb, qb) * w_intra).astype(BF16)

        va = jnp.concatenate([vo_ref[0, h * ML_DV:(h + 1) * ML_DV, :], ones_rows], axis=0)
        c_prev = c_scr[h]
        num = _dot(va, s_t) + _dot_nt(c_prev.astype(BF16), qb) * w_inter[h:h + 1, :]
        den = num[ML_DV:ML_DV + 1, :]
        h_t = num[:ML_DV, :] / jnp.maximum(jnp.abs(den), inv_floor[h:h + 1, :])

        hn = h_t * lax.rsqrt(jnp.mean(h_t * h_t, axis=0, keepdims=True) + EPS)
        hn = hn * hg_ref[h * ML_DV:(h + 1) * ML_DV, :]
        og = vo_ref[0, d + h * ML_DV:d + (h + 1) * ML_DV, :].astype(F32)
        out_ref[0, h * ML_DV:(h + 1) * ML_DV, :] = (_sigmoid(og) * hn).astype(out_ref.dtype)

        vw = (va.astype(F32) * wk[h:h + 1, :]).astype(BF16)
        c_scr[h] = decay[h:h + 1, 0:LANES] * c_prev + _dot(vw, kb)


def _mlstm(a_pre, vo_t, gates_t, conv_w, bias, hnorm_g, batch):
    m, d = a_pre.shape
    seq = m // batch
    L = ML_CHUNK
    nc = seq // L
    sub = PROJ_TM // L
    const = lambda b, c: (0, 0)
    tmin = lambda b, c: ((b * nc + c) // sub, 0, (b * nc + c) % sub)
    hg = jnp.broadcast_to(hnorm_g.reshape(d, 1), (d, L))
    bias_b = jnp.broadcast_to(bias.reshape(2 * ML_HEADS, 1), (2 * ML_HEADS, L))
    return pl.pallas_call(
        _mlstm_kernel,
        grid=(batch, nc),
        in_specs=[
            pl.BlockSpec((L, d), lambda b, c: (b * nc + c, 0)),
            pl.BlockSpec((1, 2 * d, L), tmin),
            pl.BlockSpec((1, 2 * ML_HEADS, L), tmin),
            pl.BlockSpec(conv_w.shape, const),
            pl.BlockSpec(bias_b.shape, const),
            pl.BlockSpec(hg.shape, const),
        ],
        out_specs=pl.BlockSpec((1, d, L), tmin),
        out_shape=jax.ShapeDtypeStruct((m // PROJ_TM, d, PROJ_TM), BF16),
        scratch_shapes=[
            pltpu.VMEM((SUBLANES + L, d), F32),
            pltpu.VMEM((ML_HEADS, ML_DV + ML_ONES_ROWS, LANES), F32),
            pltpu.VMEM((ML_HEADS, L), F32),
        ],
        compiler_params=_cparams(("arbitrary", "arbitrary")),
        name="mlstm_scan",
    )(a_pre, vo_t, gates_t, conv_w, bias_b, hg)


def _fox_prep_kernel(a_ref, f_ref, bf_ref, gqk_ref, place_ref, q_ref, k_ref, carry):
    ts = FX_TS
    H = FX_HEADS
    t = pl.program_id(1)

    @pl.when(t == 0)
    def _():
        carry[...] = jnp.zeros_like(carry)

    logf = _log_sigmoid(f_ref[...] + bf_ref[...]) * LOG2E
    r_i = lax.broadcasted_iota(jnp.int32, (ts, ts), 0)
    c_i = lax.broadcasted_iota(jnp.int32, (ts, ts), 1)
    tlow = jnp.where(r_i >= c_i, 1.0, 0.0).astype(BF16)
    hi, mid, lo = _split3(logf)
    fcum = _dot(tlow, hi) + _dot(tlow, mid) + _dot(tlow, lo) + carry[0:1, :]
    carry[...] = jnp.broadcast_to(fcum[ts - 1:ts, :], carry.shape)

    lane = lax.broadcasted_iota(jnp.int32, (ts, LANES), 1)
    f_m = jnp.where(lane < H, fcum, 0.0)
    f_hi = f_m.astype(BF16).astype(F32)
    f_r1 = f_m - f_hi
    f_mid = f_r1.astype(BF16).astype(F32)
    f_lo = f_r1 - f_mid
    f_parts = (f_hi + pltpu.roll(f_mid, H, 1) + pltpu.roll(f_lo, 2 * H, 1)
               + jnp.where(lane == 3 * H, 1.0, 0.0)).astype(BF16)

    first_half = lane < FX_DH
    gqk = gqk_ref[...]
    for h in range(H):
        a = a_ref[:, h * LANES:(h + 1) * LANES].astype(F32)
        lhs = jnp.concatenate([(a * a).astype(BF16), f_parts], axis=1)
        res = _dot(lhs, place_ref[h])
        rinv = lax.rsqrt(res[:, :LANES] * (1.0 / FX_DH) + EPS)
        bias = res[:, LANES:]
        a_n = a * rinv * gqk
        q_ref[:, h * LANES:(h + 1) * LANES] = jnp.where(first_half, a_n, bias).astype(BF16)
        k_ref[:, h * LANES:(h + 1) * LANES] = pltpu.roll(
            jnp.where(first_half, bias, a_n), FX_DH, 1).astype(BF16)


def _fox_placement():
    H = FX_HEADS
    r = lax.broadcasted_iota(jnp.int32, (H, 2 * LANES, 2 * LANES), 1)
    c = lax.broadcasted_iota(jnp.int32, (H, 2 * LANES, 2 * LANES), 2)
    h = lax.broadcasted_iota(jnp.int32, (H, 2 * LANES, 2 * LANES), 0)
    half = (r < LANES) & (c < LANES) & ((r < FX_DH) == (c < FX_DH))
    src = r - LANES
    dst = c - LANES
    qb = FX_BIAS_LANE
    one_src = src == 3 * H
    plus = ((src == h) & (dst == qb)) | ((src == H + h) & (dst == qb + 1)) | (
        (src == 2 * H + h) & (dst == qb + 2)) | (one_src & (dst >= qb + 3) & (dst < qb + 6)) | (
        one_src & (dst >= 0) & (dst < 3))
    minus = ((src == h) & (dst == 3)) | ((src == H + h) & (dst == 4)) | (
        (src == 2 * H + h) & (dst == 5))
    in_place = (r >= LANES) & (c >= LANES)
    val = jnp.where(half | (in_place & plus), 1.0, jnp.where(in_place & minus, -1.0, 0.0))
    return val.astype(BF16)


def _fox_prep(a, fgate, bf_row, gqk, batch):
    m = a.shape[0]
    seq = m // batch
    ts = FX_TS
    nt = seq // ts
    wide = FX_HEADS * LANES
    row = lambda b, t: (b * nt + t, 0)
    const = lambda b, t: (0, 0)
    return pl.pallas_call(
        _fox_prep_kernel,
        grid=(batch, nt),
        in_specs=[
            pl.BlockSpec((ts, wide), row),
            pl.BlockSpec((ts, LANES), row),
            pl.BlockSpec((1, LANES), const),
            pl.BlockSpec((1, LANES), const),
            pl.BlockSpec((FX_HEADS, 2 * LANES, 2 * LANES), lambda b, t: (0, 0, 0)),
        ],
        out_specs=[pl.BlockSpec((ts, wide), row)] * 2,
        out_shape=[jax.ShapeDtypeStruct((m, wide), BF16)] * 2,
        scratch_shapes=[pltpu.VMEM((SUBLANES, LANES), F32)],
        compiler_params=_cparams(("arbitrary", "arbitrary")),
        name="fox_prep",
    )(a, fgate, bf_row, gqk, _fox_placement())


def _fox_attn_kernel(qtab_ref, ktab_ref, q_ref, k_ref, vt_ref, og_ref, o_ref, m_st, acc_st,
                     *score_scr, n_diag, n_blocks):
    tq, tk = FX_TQ, FX_TK
    bufs = [(score_scr[2 * j], score_scr[2 * j + 1]) for j in range(FX_BUFS)]
    key_i = lax.broadcasted_iota(jnp.int32, (tk, tq), 0)
    qry_i = lax.broadcasted_iota(jnp.int32, (tk, tq), 1)
    causal = key_i <= qry_i
    ones_rows = jnp.ones((FX_ONES_ROWS, tk), BF16)

    def score(e, s_ref, mx_ref):
        q0 = pl.multiple_of(qtab_ref[e] * tq, tq)
        k0 = pl.multiple_of(ktab_ref[e] * tk, tk)
        for hh in range(2):
            q = q_ref[0, pl.ds(q0, tq), hh * LANES:(hh + 1) * LANES]
            k = k_ref[0, pl.ds(k0, tk), hh * LANES:(hh + 1) * LANES]
            s = _dot_nt(k, q)
            s_ref[hh] = s
            mx_ref[hh] = jnp.max(s, axis=0, keepdims=True)

    def values(kj, hh):
        return jnp.concatenate(
            [vt_ref[0, kj, hh * FX_DH:(hh + 1) * FX_DH, :], ones_rows], axis=0)

    def consume_diag(e, s_ref):
        qi = qtab_ref[e]
        for hh in range(2):
            s = jnp.where(causal, s_ref[hh], NEG)
            m_new = jnp.max(s, axis=0, keepdims=True)
            p = jnp.exp2(s - m_new).astype(BF16)
            acc_st[qi, hh] = _dot(values(qi, hh), p)
            m_st[qi, hh] = m_new

    def consume(e, s_ref, mx_ref):
        qi = qtab_ref[e]
        kj = ktab_ref[e]
        for hh in range(2):
            m_prev = m_st[qi, hh]
            m_new = jnp.maximum(m_prev, mx_ref[hh])
            p = jnp.exp2(s_ref[hh] - m_new).astype(BF16)
            corr = jnp.exp2(m_prev - m_new)
            acc_st[qi, hh] = corr * acc_st[qi, hh] + _dot(values(kj, hh), p)
            m_st[qi, hh] = m_new

    def diag_body(i, carry):
        for j in range(FX_UNROLL):
            e = FX_UNROLL * i + j
            score(e + FX_AHEAD, *bufs[(j + FX_AHEAD) % FX_BUFS])
            consume_diag(e, bufs[j % FX_BUFS][0])
        return carry

    def body(i, carry):
        for j in range(FX_UNROLL):
            e = FX_UNROLL * i + j
            score(e + FX_AHEAD, *bufs[(j + FX_AHEAD) % FX_BUFS])
            consume(e, *bufs[j % FX_BUFS])
        return carry

    for e in range(FX_AHEAD):
        score(e, *bufs[e])
    lax.fori_loop(0, n_diag // FX_UNROLL, diag_body, 0)
    lax.fori_loop(n_diag // FX_UNROLL, n_blocks // FX_UNROLL, body, 0)

    def finalize(qi, carry):
        q0 = pl.multiple_of(qi * tq, tq)
        outs = []
        for hh in range(2):
            acc = acc_st[qi, hh]
            outs.append(acc[:FX_DH, :] / acc[FX_DH:FX_DH + 1, :])
        out = jnp.concatenate(outs, axis=0).T
        og = og_ref[0, pl.ds(q0, tq), :].astype(F32)
        o_ref[0, pl.ds(q0, tq), :] = (_sigmoid(og) * out).astype(o_ref.dtype)
        return carry

    lax.fori_loop(0, n_diag, finalize, 0)


def _fox_block_tables(nq):
    qs = list(range(nq)) + [qi for qi in range(nq) for _ in range(qi)] + [0] * FX_AHEAD
    ks = list(range(nq)) + [kj for qi in range(nq) for kj in range(qi)] + [0] * FX_AHEAD
    return jnp.asarray(qs, jnp.int32), jnp.asarray(ks, jnp.int32)


def _fox_attn(q_aug, k_aug, vt, og, batch):
    m, wide = q_aug.shape
    seq = m // batch
    tq, tk = FX_TQ, FX_TK
    assert tq == tk == PROJ_TM
    nq = seq // tq
    n_blocks = nq * (nq + 1) // 2
    assert nq % FX_UNROLL == 0 and n_blocks % FX_UNROLL == 0
    pairs = FX_HEADS // 2
    q3 = q_aug.reshape(batch, seq, wide)
    k3 = k_aug.reshape(batch, seq, wide)
    vt4 = vt.reshape(batch, seq // tk, FX_HEADS * FX_DH, tk)
    og3 = og.reshape(batch, seq, FX_HEADS * FX_DH)
    qtab, ktab = _fox_block_tables(nq)
    rows = FX_DH + FX_ONES_ROWS
    grid_spec = pltpu.PrefetchScalarGridSpec(
        num_scalar_prefetch=2,
        grid=(batch, pairs),
        in_specs=[
            pl.BlockSpec((1, seq, 2 * LANES), lambda b, p, *_: (b, 0, p)),
            pl.BlockSpec((1, seq, 2 * LANES), lambda b, p, *_: (b, 0, p)),
            pl.BlockSpec((1, seq // tk, 2 * FX_DH, tk), lambda b, p, *_: (b, 0, p, 0)),
            pl.BlockSpec((1, seq, LANES), lambda b, p, *_: (b, 0, p)),
        ],
        out_specs=pl.BlockSpec((1, seq, LANES), lambda b, p, *_: (b, 0, p)),
        scratch_shapes=[pltpu.VMEM((nq, 2, 1, tq), F32),
                        pltpu.VMEM((nq, 2, rows, tq), F32)]
        + [pltpu.VMEM((2, tk, tq), F32), pltpu.VMEM((2, 1, tq), F32)] * FX_BUFS,
    )
    out = pl.pallas_call(
        functools.partial(_fox_attn_kernel, n_diag=nq, n_blocks=n_blocks),
        grid_spec=grid_spec,
        out_shape=jax.ShapeDtypeStruct((batch, seq, FX_HEADS * FX_DH), BF16),
        compiler_params=_cparams(("arbitrary", "arbitrary")),
        name="fox_attn",
    )(qtab, ktab, q3, k3, vt4, og3)
    return out.reshape(m, FX_HEADS * FX_DH)


def _pair_heads(wq, wk, heads, dh):
    lead = wq.shape[0]
    return jnp.concatenate(
        [wq.reshape(lead, heads, dh), wk.reshape(lead, heads, dh)], axis=2
    ).reshape(lead, heads * 2 * dh)


def _pad_lanes(w):
    return jnp.zeros((w.shape[0], LANES), w.dtype).at[:, :w.shape[1]].set(w)


def kernel(x, c, ada_w, ada_b, norm_g, ffn_w_in, ffn_w_out, ml_w_in, ml_conv_w, ml_b_i, ml_b_f,
           ml_hnorm_g, ml_w_out, fx_w_in, fx_b_f, fx_qnorm_g, fx_knorm_g, fx_w_out, final_g):
    batch, seq, d = x.shape
    assert d == D_MODEL and seq % FFN_TM == 0 and seq % FX_TQ == 0 and seq % ML_CHUNK == 0
    depth = ada_w.shape[0]
    assert depth == 2 and batch <= SUBLANES

    mod = _modulation(c, ada_w, ada_b)
    w_in_b = ffn_w_in.astype(BF16)
    w_out_b = ffn_w_out.astype(BF16)
    x2 = x.reshape(batch * seq, d)

    x2 = _ffn(x2, mod, 0, 0, batch, norm_g[0, 0], w_in_b[0, 0], w_out_b[0, 0])

    H, dk, dv = ML_HEADS, ML_DQK, ML_DV
    w = ml_w_in[0]
    n_qk = H * dk
    w_a = _pair_heads(w[:, :n_qk], w[:, n_qk:2 * n_qk], H, dk)
    w_v = w[:, 2 * n_qk:2 * n_qk + H * dv]
    w_o = w[:, 2 * n_qk + H * dv:2 * n_qk + 2 * H * dv]
    w_g = w[:, 2 * n_qk + 2 * H * dv:]
    w_vo_t = jnp.concatenate([w_v, w_o], axis=1).T.astype(BF16)
    cw = ml_conv_w[0]
    conv_w = _pair_heads(cw[:, :n_qk], cw[:, n_qk:], H, dk)
    bias = jnp.concatenate([ml_b_i[0], ml_b_f[0]])
    a_pre, vo_t, gates_t = _proj(
        x2, mod, 0, batch, norm_g[0, 1], w_a.astype(BF16), (2 * n_qk,), (BF16,),
        (w_vo_t, w_g.T.astype(BF16)), (BF16, F32))
    hs_t = _mlstm(a_pre, vo_t, gates_t, conv_w, bias, ml_hnorm_g[0], batch)
    x2 = _ffn(x2, mod, 0, 2, batch, norm_g[0, 2], w_in_b[0, 1], w_out_b[0, 1],
              mixer_out=hs_t, w_mix=ml_w_out[0].astype(BF16))

    x2 = _ffn(x2, mod, 1, 0, batch, norm_g[1, 0], w_in_b[1, 0], w_out_b[1, 0])

    H, dh = FX_HEADS, FX_DH
    w = fx_w_in[0]
    n = H * dh
    w_a = _pair_heads(w[:, :n], w[:, n:2 * n], H, dh)
    w_main = jnp.concatenate([w_a, w[:, 3 * n:4 * n], _pad_lanes(w[:, 4 * n:])], axis=1)
    a, og, fgate, vt = _proj(
        x2, mod, 1, batch, norm_g[1, 1], w_main.astype(BF16), (2 * n, n, LANES),
        (BF16, BF16, F32), (w[:, 2 * n:3 * n].T.astype(BF16),), (BF16,))
    gqk = jnp.concatenate(
        [fx_qnorm_g[0] * (dh ** -0.5 * LOG2E), fx_knorm_g[0]]).reshape(1, LANES)
    bf_row = _pad_lanes(fx_b_f[0].reshape(1, H))
    q_aug, k_aug = _fox_prep(a, fgate, bf_row, gqk, batch)
    attn = _fox_attn(q_aug, k_aug, vt, og, batch)
    x2 = _ffn(x2, mod, 1, 2, batch, norm_g[1, 2], w_in_b[1, 1], w_out_b[1, 1], final_g=final_g,
              mixer_out=attn, w_mix=fx_w_out[0].astype(BF16))
    return x2.reshape(batch, seq, d)
```

```python
import functools

import jax
import jax.numpy as jnp
from jax import lax
from jax.experimental import pallas as pl
from jax.experimental.pallas import tpu as pltpu

D_MODEL = 1024
D_FF = 4 * D_MODEL
EPS = 1e-6
FFN_RES = 0.5
CONV_W = 4
ML_HEADS = 8
ML_DV = 128
ML_DQK = 64
FX_HEADS = 16
FX_DH = 64
NEG = -1e30

LANES = 128
SUBLANES = 8
VMEM_LIMIT_BYTES = 56 * 1024 * 1024

FFN_TM = 1024
FFN_TF = 1024
PROJ_TM = 512
PROJ_TN = 512
ML_CHUNK = 256
ML_ONES_ROWS = 16
FX_TS = 512
FX_TQ = 512
FX_TK = 512
FX_BIAS_LANE = FX_DH
FX_PRUNE_GAP = 170.0
FX_BOUND_SLACK = 1.05
FX_AHEAD = 1
FX_BUFS = FX_AHEAD + 1
FX_UNROLL = 8
FX_ONES_ROWS = 16
LOG2E = 1.4426950408889634

BF16 = jnp.bfloat16
F32 = jnp.float32


def _cparams(sem):
    return pltpu.CompilerParams(dimension_semantics=sem, vmem_limit_bytes=VMEM_LIMIT_BYTES)


def _dot(a, b):
    return jnp.dot(a, b, preferred_element_type=F32)


def _dot_nt(a, b):
    return lax.dot_general(a, b, (((1,), (1,)), ((), ())), preferred_element_type=F32)


def _dot_tn(a, b):
    return lax.dot_general(a, b, (((0,), (0,)), ((), ())), preferred_element_type=F32)


def _sigmoid(x):
    return 1.0 / (1.0 + jnp.exp(-x))


def _log_sigmoid(x):
    return jnp.minimum(x, 0.0) - jnp.log(1.0 + jnp.exp(-jnp.abs(x)))


def _split3(x):
    hi = x.astype(BF16)
    r1 = x - hi.astype(F32)
    mid = r1.astype(BF16)
    lo = (r1 - mid.astype(F32)).astype(BF16)
    return hi, mid, lo


def _adaln(x, g, shift, scale):
    ms = jnp.mean(x * x, axis=-1, keepdims=True)
    return (x * lax.rsqrt(ms + EPS) * g) * (1.0 + scale) + shift


def _mod_kernel(c_ref, w_ref, b_ref, o_ref):
    c = c_ref[...]
    cond = (c * _sigmoid(c)).astype(BF16)
    o_ref[0] = _dot(cond, w_ref[0].astype(BF16)) + b_ref[0]


def _modulation(c, ada_w, ada_b):
    depth, d, n = ada_w.shape
    b = c.shape[0]
    c_pad = jnp.zeros((SUBLANES, d), F32).at[:b].set(c)
    tn = 1024
    out = pl.pallas_call(
        _mod_kernel,
        grid=(depth, n // tn),
        in_specs=[
            pl.BlockSpec((SUBLANES, d), lambda l, j: (0, 0)),
            pl.BlockSpec((1, d, tn), lambda l, j: (l, 0, j)),
            pl.BlockSpec((1, 1, tn), lambda l, j: (l, 0, j)),
        ],
        out_specs=pl.BlockSpec((1, SUBLANES, tn), lambda l, j: (l, 0, j)),
        out_shape=jax.ShapeDtypeStruct((depth, SUBLANES, n), F32),
        compiler_params=_cparams(("arbitrary", "arbitrary")),
        name="adaln_mod",
    )(c_pad, ada_w, ada_b.reshape(depth, 1, n))
    mod = out[:, :b].reshape(depth, b, 9, d).transpose(0, 2, 1, 3)
    return mod.reshape(depth * 9 * b, 1, d)


def _mod_spec(layer, idx, batch, rows_per_batch_tiles):
    base = (layer * 9 + idx) * batch
    return pl.BlockSpec((None, 1, D_MODEL),
                        lambda m, *_: (base + m // rows_per_batch_tiles, 0, 0))


def _ffn_kernel(*refs, final_norm, mixer):
    refs = list(refs)
    x_ref, sh_ref, sc_ref, gt_ref, g_ref, wg_ref, wu_ref, wo_ref = refs[:8]
    del refs[:8]
    fg_ref = refs.pop(0) if final_norm else None
    if mixer is not None:
        a_ref, wmix_ref, gmix_ref = refs[:3]
        del refs[:3]
        o_ref, h_scr, acc_scr, x1_scr = refs
    else:
        o_ref, h_scr, acc_scr = refs
    f = pl.program_id(1)

    @pl.when(f == 0)
    def _():
        x1 = x_ref[...]
        if mixer == "rows":
            y = _dot(a_ref[...], wmix_ref[...])
        elif mixer == "time_minor":
            y = jnp.concatenate(
                [_dot_tn(a_ref[j], wmix_ref[...]) for j in range(a_ref.shape[0])], axis=0)
        if mixer is not None:
            x1 = x1 + (1.0 + gmix_ref[...]) * y
            x1_scr[...] = x1
        h = _adaln(x1, g_ref[...], sh_ref[...], sc_ref[...])
        h_scr[...] = h.astype(BF16)
        acc_scr[...] = jnp.zeros_like(acc_scr)

    h = h_scr[...]
    gate = _dot(h, wg_ref[...])
    up = _dot(h, wu_ref[...])
    a = (gate * _sigmoid(gate) * up).astype(BF16)
    acc_scr[...] += _dot(a, wo_ref[...])

    @pl.when(f == pl.num_programs(1) - 1)
    def _():
        x1 = x_ref[...] if mixer is None else x1_scr[...]
        out = x1 + (FFN_RES * (1.0 + gt_ref[...])) * acc_scr[...]
        if final_norm:
            ms = jnp.mean(out * out, axis=-1, keepdims=True)
            out = out * lax.rsqrt(ms + EPS) * fg_ref[...]
        o_ref[...] = out


def _ffn(x2, mod, layer, sub, batch, g, w_in, w_out, final_g=None, mixer_out=None, w_mix=None):
    m, d = x2.shape
    f_dim = w_out.shape[0]
    tm, tf = FFN_TM, FFN_TF
    nf = f_dim // tf
    tiles_per_batch = (m // batch) // tm
    final_norm = final_g is not None
    row = lambda m_, f_: (m_, 0)
    const = lambda m_, f_: (0, 0)
    in_specs = [
        pl.BlockSpec((tm, d), row),
        _mod_spec(layer, sub * 3 + 0, batch, tiles_per_batch),
        _mod_spec(layer, sub * 3 + 1, batch, tiles_per_batch),
        _mod_spec(layer, sub * 3 + 2, batch, tiles_per_batch),
        pl.BlockSpec((1, d), const),
        pl.BlockSpec((d, tf), lambda m_, f_: (0, f_)),
        pl.BlockSpec((d, tf), lambda m_, f_: (0, nf + f_)),
        pl.BlockSpec((tf, d), lambda m_, f_: (f_, 0)),
    ]
    args = [x2, mod, mod, mod, g.reshape(1, d), w_in, w_in, w_out]
    scratch = [pltpu.VMEM((tm, d), BF16), pltpu.VMEM((tm, d), F32)]
    if final_norm:
        in_specs.append(pl.BlockSpec((1, d), const))
        args.append(final_g.reshape(1, d))
    mixer = None
    if mixer_out is not None:
        if mixer_out.ndim == 3:
            mixer = "time_minor"
            sub_tiles = tm // PROJ_TM
            in_specs.append(pl.BlockSpec((sub_tiles, mixer_out.shape[1], PROJ_TM),
                                         lambda m_, f_: (m_, 0, 0)))
        else:
            mixer = "rows"
            in_specs.append(pl.BlockSpec((tm, mixer_out.shape[1]), row))
        in_specs += [pl.BlockSpec(w_mix.shape, const),
                     _mod_spec(layer, 5, batch, tiles_per_batch)]
        args += [mixer_out, w_mix, mod]
        scratch.append(pltpu.VMEM((tm, d), F32))
    return pl.pallas_call(
        functools.partial(_ffn_kernel, final_norm=final_norm, mixer=mixer),
        grid=(m // tm, nf),
        in_specs=in_specs,
        out_specs=pl.BlockSpec((tm, d), row),
        out_shape=jax.ShapeDtypeStruct((m, d), F32),
        scratch_shapes=scratch,
        compiler_params=_cparams(("arbitrary", "arbitrary")),
        name="ffn_final" if final_norm else ("ffn" if mixer is None else "ffn_mix"),
    )(*args)


def _proj_kernel(x_ref, sh_ref, sc_ref, g_ref, w_ref, *refs, widths, n_t):
    h = _adaln(x_ref[...], g_ref[...], sh_ref[...], sc_ref[...]).astype(BF16)
    wt_refs = refs[:n_t]
    row_refs = refs[n_t:n_t + len(widths)]
    t_refs = refs[n_t + len(widths):]
    col = 0
    for o_ref, width in zip(row_refs, widths):
        step = min(width, PROJ_TN)
        for c0 in range(0, width, step):
            o_ref[:, c0:c0 + step] = _dot(
                h, w_ref[:, col + c0:col + c0 + step]).astype(o_ref.dtype)
        col += width
    for wt_ref, t_ref in zip(wt_refs, t_refs):
        rows = wt_ref.shape[0]
        step = min(rows, PROJ_TN)
        for r0 in range(0, rows, step):
            t_ref[0, r0:r0 + step, :] = _dot_nt(wt_ref[r0:r0 + step, :], h).astype(t_ref.dtype)


def _proj(x2, mod, layer, batch, g, w_main, widths, row_dtypes, w_ts, t_dtypes):
    m, d = x2.shape
    tm = PROJ_TM
    tiles_per_batch = (m // batch) // tm
    row = lambda m_: (m_, 0)
    const = lambda m_: (0, 0)
    out_shape = [jax.ShapeDtypeStruct((m, w), dt) for w, dt in zip(widths, row_dtypes)]
    out_specs = [pl.BlockSpec((tm, w), row) for w in widths]
    for w_t, dt in zip(w_ts, t_dtypes):
        out_shape.append(jax.ShapeDtypeStruct((m // tm, w_t.shape[0], tm), dt))
        out_specs.append(pl.BlockSpec((1, w_t.shape[0], tm), lambda m_: (m_, 0, 0)))
    return pl.pallas_call(
        functools.partial(_proj_kernel, widths=tuple(widths), n_t=len(w_ts)),
        grid=(m // tm,),
        in_specs=[
            pl.BlockSpec((tm, d), row),
            _mod_spec(layer, 3, batch, tiles_per_batch),
            _mod_spec(layer, 4, batch, tiles_per_batch),
            pl.BlockSpec((1, d), const),
            pl.BlockSpec(w_main.shape, const),
        ] + [pl.BlockSpec(w_t.shape, const) for w_t in w_ts],
        out_specs=out_specs,
        out_shape=out_shape,
        compiler_params=_cparams(("arbitrary",)),
        name="mixer_proj",
    )(x2, mod, mod, g.reshape(1, d), w_main, *w_ts)


def _mlstm_kernel(a_ref, vo_ref, gt_ref, cw_ref, bias_ref, hg_ref, out_ref, xbuf, c_scr, m_scr):
    L = ML_CHUNK
    H = ML_HEADS
    d = a_ref.shape[1]
    c = pl.program_id(1)

    @pl.when(c == 0)
    def _():
        xbuf[0:SUBLANES, :] = jnp.zeros((SUBLANES, xbuf.shape[1]), F32)
        c_scr[...] = jnp.zeros_like(c_scr)
        m_scr[...] = jnp.zeros_like(m_scr)

    xbuf[SUBLANES:SUBLANES + L, :] = a_ref[...].astype(F32)
    y = cw_ref[CONV_W - 1:CONV_W, :] * xbuf[SUBLANES:SUBLANES + L, :]
    for j in range(CONV_W - 1):
        off = SUBLANES - (CONV_W - 1) + j
        y = y + cw_ref[j:j + 1, :] * xbuf[off:off + L, :]
    xbuf[0:SUBLANES, :] = xbuf[L:L + SUBLANES, :]
    act = y * _sigmoid(y)

    gt = gt_ref[0] + bias_ref[...]
    lf_all = _log_sigmoid(gt) * LOG2E
    s_i = lax.broadcasted_iota(jnp.int32, (L, L), 0)
    t_i = lax.broadcasted_iota(jnp.int32, (L, L), 1)
    causal_t = s_i <= t_i
    tup = jnp.where(causal_t, 1.0, 0.0).astype(BF16)
    r_hi, r_mid, r_lo = _split3(lf_all)
    b = (_dot(r_hi, tup) + _dot(r_mid, tup) + _dot(r_lo, tup))[H:2 * H]
    i2 = gt[0:H] * LOG2E
    rowv = i2 - b
    lane = lax.broadcasted_iota(jnp.int32, (H, L), 1)
    cmax = rowv
    shift = 1
    while shift < L:
        cmax = jnp.maximum(cmax, jnp.where(lane >= shift, pltpu.roll(cmax, shift, 1), NEG))
        shift *= 2
    m_prev = m_scr[...]
    u = jnp.maximum(m_prev, cmax)
    w_inter = jnp.exp2(m_prev - u)
    inv_floor = jnp.exp2(-(b + u))
    g_tot = jnp.broadcast_to(b[:, L - 1:L], (H, L))
    a_w = g_tot - b + i2
    m_new = jnp.maximum(g_tot + m_prev, jnp.max(a_w, axis=1, keepdims=True))
    wk = jnp.exp2(a_w - m_new)
    decay = jnp.exp2(g_tot + m_prev - m_new)
    m_scr[...] = m_new

    rv_parts = [p.astype(F32) for p in _split3(rowv)]
    nu_parts = [p.astype(F32) for p in _split3(-u)]
    sub16 = lax.broadcasted_iota(jnp.int32, (2 * SUBLANES, L), 0)
    row16 = lambda x, h_: jnp.broadcast_to(x[h_:h_ + 1, :], (2 * SUBLANES, L))

    lane128 = lax.broadcasted_iota(jnp.int32, (L, LANES), 1)
    first_half = lane128 < ML_DQK
    ones_rows = jnp.ones((ML_ONES_ROWS, L), BF16)

    for h in range(H):
        a_h = act[:, h * LANES:(h + 1) * LANES]
        qb = (jnp.where(first_half, a_h, 0.0) * (ML_DQK ** -0.5)).astype(BF16)
        kb = jnp.where(first_half, pltpu.roll(a_h, ML_DQK, 1), 0.0).astype(BF16)

        src = jnp.where(sub16 == 0, row16(rv_parts[0], h), jnp.where(
            sub16 == 1, row16(rv_parts[1], h), jnp.where(
                sub16 == 2, row16(rv_parts[2], h), jnp.where(sub16 < 6, 1.0, 0.0))))
        dst = jnp.where(sub16 < 3, 1.0, jnp.where(
            sub16 == 3, row16(nu_parts[0], h), jnp.where(
                sub16 == 4, row16(nu_parts[1], h), jnp.where(
                    sub16 == 5, row16(nu_parts[2], h), 0.0))))
        z = _dot_tn(src.astype(BF16), dst.astype(BF16))
        w_intra = jnp.exp2(jnp.where(causal_t, z, NEG))
        s_t = (_dot_nt(kb, qb) * w_intra).astype(BF16)

        va = jnp.concatenate([vo_ref[0, h * ML_DV:(h + 1) * ML_DV, :], ones_rows], axis=0)
        c_prev = c_scr[h]
        num = _dot(va, s_t) + _dot_nt(c_prev.astype(BF16), qb) * w_inter[h:h + 1, :]
        den = num[ML_DV:ML_DV + 1, :]
        h_t = num[:ML_DV, :] / jnp.maximum(jnp.abs(den), inv_floor[h:h + 1, :])

        hn = h_t * lax.rsqrt(jnp.mean(h_t * h_t, axis=0, keepdims=True) + EPS)
        hn = hn * hg_ref[h * ML_DV:(h + 1) * ML_DV, :]
        og = vo_ref[0, d + h * ML_DV:d + (h + 1) * ML_DV, :].astype(F32)
        out_ref[0, h * ML_DV:(h + 1) * ML_DV, :] = (_sigmoid(og) * hn).astype(out_ref.dtype)

        vw = (va.astype(F32) * wk[h:h + 1, :]).astype(BF16)
        c_scr[h] = decay[h:h + 1, 0:LANES] * c_prev + _dot(vw, kb)


def _mlstm(a_pre, vo_t, gates_t, conv_w, bias, hnorm_g, batch):
    m, d = a_pre.shape
    seq = m // batch
    L = ML_CHUNK
    nc = seq // L
    sub = PROJ_TM // L
    const = lambda b, c: (0, 0)
    tmin = lambda b, c: ((b * nc + c) // sub, 0, (b * nc + c) % sub)
    hg = jnp.broadcast_to(hnorm_g.reshape(d, 1), (d, L))
    bias_b = jnp.broadcast_to(bias.reshape(2 * ML_HEADS, 1), (2 * ML_HEADS, L))
    return pl.pallas_call(
        _mlstm_kernel,
        grid=(batch, nc),
        in_specs=[
            pl.BlockSpec((L, d), lambda b, c: (b * nc + c, 0)),
            pl.BlockSpec((1, 2 * d, L), tmin),
            pl.BlockSpec((1, 2 * ML_HEADS, L), tmin),
            pl.BlockSpec(conv_w.shape, const),
            pl.BlockSpec(bias_b.shape, const),
            pl.BlockSpec(hg.shape, const),
        ],
        out_specs=pl.BlockSpec((1, d, L), tmin),
        out_shape=jax.ShapeDtypeStruct((m // PROJ_TM, d, PROJ_TM), BF16),
        scratch_shapes=[
            pltpu.VMEM((SUBLANES + L, d), F32),
            pltpu.VMEM((ML_HEADS, ML_DV + ML_ONES_ROWS, LANES), F32),
            pltpu.VMEM((ML_HEADS, L), F32),
        ],
        compiler_params=_cparams(("arbitrary", "arbitrary")),
        name="mlstm_scan",
    )(a_pre, vo_t, gates_t, conv_w, bias_b, hg)


def _fox_prep_kernel(a_ref, f_ref, bf_ref, gqk_ref, place_ref, q_ref, k_ref, ends_ref, carry):
    ts = FX_TS
    H = FX_HEADS
    t = pl.program_id(1)

    @pl.when(t == 0)
    def _():
        carry[...] = jnp.zeros_like(carry)

    logf = _log_sigmoid(f_ref[...] + bf_ref[...]) * LOG2E
    r_i = lax.broadcasted_iota(jnp.int32, (ts, ts), 0)
    c_i = lax.broadcasted_iota(jnp.int32, (ts, ts), 1)
    tlow = jnp.where(r_i >= c_i, 1.0, 0.0).astype(BF16)
    hi, mid, lo = _split3(logf)
    fcum = _dot(tlow, hi) + _dot(tlow, mid) + _dot(tlow, lo) + carry[0:1, :]
    carry[...] = jnp.broadcast_to(fcum[ts - 1:ts, :], carry.shape)

    sub = lax.broadcasted_iota(jnp.int32, (SUBLANES, LANES), 0)
    ends_ref[0] = jnp.where(
        sub == 0, jnp.broadcast_to(fcum[0:1, :], (SUBLANES, LANES)),
        jnp.broadcast_to(fcum[ts - 1:ts, :], (SUBLANES, LANES)))

    lane = lax.broadcasted_iota(jnp.int32, (ts, LANES), 1)
    f_m = jnp.where(lane < H, fcum, 0.0)
    f_hi = f_m.astype(BF16).astype(F32)
    f_r1 = f_m - f_hi
    f_mid = f_r1.astype(BF16).astype(F32)
    f_lo = f_r1 - f_mid
    f_parts = (f_hi + pltpu.roll(f_mid, H, 1) + pltpu.roll(f_lo, 2 * H, 1)
               + jnp.where(lane == 3 * H, 1.0, 0.0)).astype(BF16)

    first_half = lane < FX_DH
    gqk = gqk_ref[...]
    for h in range(H):
        a = a_ref[:, h * LANES:(h + 1) * LANES].astype(F32)
        lhs = jnp.concatenate([(a * a).astype(BF16), f_parts], axis=1)
        res = _dot(lhs, place_ref[h])
        rinv = lax.rsqrt(res[:, :LANES] * (1.0 / FX_DH) + EPS)
        bias = res[:, LANES:]
        a_n = a * rinv * gqk
        q_ref[:, h * LANES:(h + 1) * LANES] = jnp.where(first_half, a_n, bias).astype(BF16)
        k_ref[:, h * LANES:(h + 1) * LANES] = pltpu.roll(
            jnp.where(first_half, bias, a_n), FX_DH, 1).astype(BF16)


def _fox_placement():
    H = FX_HEADS
    r = lax.broadcasted_iota(jnp.int32, (H, 2 * LANES, 2 * LANES), 1)
    c = lax.broadcasted_iota(jnp.int32, (H, 2 * LANES, 2 * LANES), 2)
    h = lax.broadcasted_iota(jnp.int32, (H, 2 * LANES, 2 * LANES), 0)
    half = (r < LANES) & (c < LANES) & ((r < FX_DH) == (c < FX_DH))
    src = r - LANES
    dst = c - LANES
    qb = FX_BIAS_LANE
    one_src = src == 3 * H
    plus = ((src == h) & (dst == qb)) | ((src == H + h) & (dst == qb + 1)) | (
        (src == 2 * H + h) & (dst == qb + 2)) | (one_src & (dst >= qb + 3) & (dst < qb + 6)) | (
        one_src & (dst >= 0) & (dst < 3))
    minus = ((src == h) & (dst == 3)) | ((src == H + h) & (dst == 4)) | (
        (src == 2 * H + h) & (dst == 5))
    in_place = (r >= LANES) & (c >= LANES)
    val = jnp.where(half | (in_place & plus), 1.0, jnp.where(in_place & minus, -1.0, 0.0))
    return val.astype(BF16)


def _fox_prep(a, fgate, bf_row, gqk, batch):
    m = a.shape[0]
    seq = m // batch
    ts = FX_TS
    nt = seq // ts
    wide = FX_HEADS * LANES
    row = lambda b, t: (b * nt + t, 0)
    const = lambda b, t: (0, 0)
    return pl.pallas_call(
        _fox_prep_kernel,
        grid=(batch, nt),
        in_specs=[
            pl.BlockSpec((ts, wide), row),
            pl.BlockSpec((ts, LANES), row),
            pl.BlockSpec((1, LANES), const),
            pl.BlockSpec((1, LANES), const),
            pl.BlockSpec((FX_HEADS, 2 * LANES, 2 * LANES), lambda b, t: (0, 0, 0)),
        ],
        out_specs=[pl.BlockSpec((ts, wide), row)] * 2
        + [pl.BlockSpec((1, SUBLANES, LANES), lambda b, t: (b * nt + t, 0, 0))],
        out_shape=[jax.ShapeDtypeStruct((m, wide), BF16)] * 2
        + [jax.ShapeDtypeStruct((m // ts, SUBLANES, LANES), F32)],
        scratch_shapes=[pltpu.VMEM((SUBLANES, LANES), F32)],
        compiler_params=_cparams(("arbitrary", "arbitrary")),
        name="fox_prep",
    )(a, fgate, bf_row, gqk, _fox_placement())


def _fox_attn_kernel(qtab_ref, ktab_ref, nit_ref, q_ref, k_ref, vt_ref, og_ref, o_ref, m_st,
                     acc_st, *score_scr, n_diag, stride):
    tq, tk = FX_TQ, FX_TK
    step = pl.program_id(0) * pl.num_programs(1) + pl.program_id(1)
    base = step * stride
    bufs = [(score_scr[2 * j], score_scr[2 * j + 1]) for j in range(FX_BUFS)]
    key_i = lax.broadcasted_iota(jnp.int32, (tk, tq), 0)
    qry_i = lax.broadcasted_iota(jnp.int32, (tk, tq), 1)
    causal = key_i <= qry_i
    ones_rows = jnp.ones((FX_ONES_ROWS, tk), BF16)

    def score(e, s_ref, mx_ref):
        q0 = pl.multiple_of(qtab_ref[base + e] * tq, tq)
        k0 = pl.multiple_of(ktab_ref[base + e] * tk, tk)
        for hh in range(2):
            q = q_ref[0, pl.ds(q0, tq), hh * LANES:(hh + 1) * LANES]
            k = k_ref[0, pl.ds(k0, tk), hh * LANES:(hh + 1) * LANES]
            s = _dot_nt(k, q)
            s_ref[hh] = s
            mx_ref[hh] = jnp.max(s, axis=0, keepdims=True)

    def values(kj, hh):
        return jnp.concatenate(
            [vt_ref[0, kj, hh * FX_DH:(hh + 1) * FX_DH, :], ones_rows], axis=0)

    def consume_diag(e, s_ref):
        qi = qtab_ref[base + e]
        for hh in range(2):
            s = jnp.where(causal, s_ref[hh], NEG)
            m_new = jnp.max(s, axis=0, keepdims=True)
            p = jnp.exp2(s - m_new).astype(BF16)
            acc_st[qi, hh] = _dot(values(qi, hh), p)
            m_st[qi, hh] = m_new

    def consume(e, s_ref, mx_ref):
        qi = qtab_ref[base + e]
        kj = ktab_ref[base + e]
        for hh in range(2):
            m_prev = m_st[qi, hh]
            m_new = jnp.maximum(m_prev, mx_ref[hh])
            p = jnp.exp2(s_ref[hh] - m_new).astype(BF16)
            corr = jnp.exp2(m_prev - m_new)
            acc_st[qi, hh] = corr * acc_st[qi, hh] + _dot(values(kj, hh), p)
            m_st[qi, hh] = m_new

    def diag_body(i, carry):
        for j in range(FX_UNROLL):
            e = FX_UNROLL * i + j
            score(e + FX_AHEAD, *bufs[(j + FX_AHEAD) % FX_BUFS])
            consume_diag(e, bufs[j % FX_BUFS][0])
        return carry

    def body(i, carry):
        for j in range(FX_UNROLL):
            e = FX_UNROLL * i + j
            score(e + FX_AHEAD, *bufs[(j + FX_AHEAD) % FX_BUFS])
            consume(e, *bufs[j % FX_BUFS])
        return carry

    for e in range(FX_AHEAD):
        score(e, *bufs[e])
    lax.fori_loop(0, n_diag // FX_UNROLL, diag_body, 0)
    lax.fori_loop(n_diag // FX_UNROLL, n_diag // FX_UNROLL + nit_ref[step], body, 0)

    def finalize(qi, carry):
        q0 = pl.multiple_of(qi * tq, tq)
        outs = []
        for hh in range(2):
            acc = acc_st[qi, hh]
            outs.append(acc[:FX_DH, :] / acc[FX_DH:FX_DH + 1, :])
        out = jnp.concatenate(outs, axis=0).T
        og = og_ref[0, pl.ds(q0, tq), :].astype(F32)
        o_ref[0, pl.ds(q0, tq), :] = (_sigmoid(og) * out).astype(o_ref.dtype)
        return carry

    lax.fori_loop(0, n_diag, finalize, 0)


def _fox_block_tables(f_ends, qk_bound, batch, nq):
    pairs = FX_HEADS // 2
    f_first = f_ends[:, 0, :FX_HEADS].reshape(batch, nq, pairs, 2)
    f_last = f_ends[:, 1, :FX_HEADS].reshape(batch, nq, pairs, 2)
    qi = jnp.asarray([i for i in range(nq) for _ in range(i)], jnp.int32)
    kj = jnp.asarray([j for i in range(nq) for j in range(i)], jnp.int32)
    gap = f_last[:, kj] - f_first[:, qi]
    dead = jnp.all(gap > 2.0 * qk_bound + FX_PRUNE_GAP, axis=-1)
    dead = dead.transpose(0, 2, 1)
    order = jnp.argsort(dead.astype(jnp.int32), axis=-1, stable=True)
    n_live = jnp.sum(jnp.logical_not(dead), axis=-1).astype(jnp.int32)
    n_iter = (n_live + FX_UNROLL - 1) // FX_UNROLL
    diag = jnp.broadcast_to(jnp.arange(nq, dtype=jnp.int32), (batch, pairs, nq))
    pad = jnp.zeros((batch, pairs, FX_AHEAD), jnp.int32)
    qtab = jnp.concatenate([diag, qi[order], pad], axis=-1)
    ktab = jnp.concatenate([diag, kj[order], pad], axis=-1)
    return qtab.reshape(-1), ktab.reshape(-1), n_iter.reshape(-1), qtab.shape[-1]


def _fox_attn(q_aug, k_aug, vt, og, f_ends, qk_bound, batch):
    m, wide = q_aug.shape
    seq = m // batch
    tq, tk = FX_TQ, FX_TK
    assert tq == tk == PROJ_TM == FX_TS
    nq = seq // tq
    assert nq % FX_UNROLL == 0 and (nq * (nq - 1) // 2) % FX_UNROLL == 0
    pairs = FX_HEADS // 2
    q3 = q_aug.reshape(batch, seq, wide)
    k3 = k_aug.reshape(batch, seq, wide)
    vt4 = vt.reshape(batch, seq // tk, FX_HEADS * FX_DH, tk)
    og3 = og.reshape(batch, seq, FX_HEADS * FX_DH)
    qtab, ktab, n_iter, stride = _fox_block_tables(f_ends, qk_bound, batch, nq)
    rows = FX_DH + FX_ONES_ROWS
    grid_spec = pltpu.PrefetchScalarGridSpec(
        num_scalar_prefetch=3,
        grid=(batch, pairs),
        in_specs=[
            pl.BlockSpec((1, seq, 2 * LANES), lambda b, p, *_: (b, 0, p)),
            pl.BlockSpec((1, seq, 2 * LANES), lambda b, p, *_: (b, 0, p)),
            pl.BlockSpec((1, seq // tk, 2 * FX_DH, tk), lambda b, p, *_: (b, 0, p, 0)),
            pl.BlockSpec((1, seq, LANES), lambda b, p, *_: (b, 0, p)),
        ],
        out_specs=pl.BlockSpec((1, seq, LANES), lambda b, p, *_: (b, 0, p)),
        scratch_shapes=[pltpu.VMEM((nq, 2, 1, tq), F32),
                        pltpu.VMEM((nq, 2, rows, tq), F32)]
        + [pltpu.VMEM((2, tk, tq), F32), pltpu.VMEM((2, 1, tq), F32)] * FX_BUFS,
    )
    out = pl.pallas_call(
        functools.partial(_fox_attn_kernel, n_diag=nq, stride=stride),
        grid_spec=grid_spec,
        out_shape=jax.ShapeDtypeStruct((batch, seq, FX_HEADS * FX_DH), BF16),
        compiler_params=_cparams(("arbitrary", "arbitrary")),
        name="fox_attn",
    )(qtab, ktab, n_iter, q3, k3, vt4, og3)
    return out.reshape(m, FX_HEADS * FX_DH)


def _pair_heads(wq, wk, heads, dh):
    lead = wq.shape[0]
    return jnp.concatenate(
        [wq.reshape(lead, heads, dh), wk.reshape(lead, heads, dh)], axis=2
    ).reshape(lead, heads * 2 * dh)


def _pad_lanes(w):
    return jnp.zeros((w.shape[0], LANES), w.dtype).at[:, :w.shape[1]].set(w)


def kernel(x, c, ada_w, ada_b, norm_g, ffn_w_in, ffn_w_out, ml_w_in, ml_conv_w, ml_b_i, ml_b_f,
           ml_hnorm_g, ml_w_out, fx_w_in, fx_b_f, fx_qnorm_g, fx_knorm_g, fx_w_out, final_g):
    batch, seq, d = x.shape
    assert d == D_MODEL and seq % FFN_TM == 0 and seq % FX_TQ == 0 and seq % ML_CHUNK == 0
    depth = ada_w.shape[0]
    assert depth == 2 and batch <= SUBLANES

    mod = _modulation(c, ada_w, ada_b)
    w_in_b = ffn_w_in.astype(BF16)
    w_out_b = ffn_w_out.astype(BF16)
    x2 = x.reshape(batch * seq, d)

    x2 = _ffn(x2, mod, 0, 0, batch, norm_g[0, 0], w_in_b[0, 0], w_out_b[0, 0])

    H, dk, dv = ML_HEADS, ML_DQK, ML_DV
    w = ml_w_in[0]
    n_qk = H * dk
    w_a = _pair_heads(w[:, :n_qk], w[:, n_qk:2 * n_qk], H, dk)
    w_v = w[:, 2 * n_qk:2 * n_qk + H * dv]
    w_o = w[:, 2 * n_qk + H * dv:2 * n_qk + 2 * H * dv]
    w_g = w[:, 2 * n_qk + 2 * H * dv:]
    w_vo_t = jnp.concatenate([w_v, w_o], axis=1).T.astype(BF16)
    cw = ml_conv_w[0]
    conv_w = _pair_heads(cw[:, :n_qk], cw[:, n_qk:], H, dk)
    bias = jnp.concatenate([ml_b_i[0], ml_b_f[0]])
    a_pre, vo_t, gates_t = _proj(
        x2, mod, 0, batch, norm_g[0, 1], w_a.astype(BF16), (2 * n_qk,), (BF16,),
        (w_vo_t, w_g.T.astype(BF16)), (BF16, F32))
    hs_t = _mlstm(a_pre, vo_t, gates_t, conv_w, bias, ml_hnorm_g[0], batch)
    x2 = _ffn(x2, mod, 0, 2, batch, norm_g[0, 2], w_in_b[0, 1], w_out_b[0, 1],
              mixer_out=hs_t, w_mix=ml_w_out[0].astype(BF16))

    x2 = _ffn(x2, mod, 1, 0, batch, norm_g[1, 0], w_in_b[1, 0], w_out_b[1, 0])

    H, dh = FX_HEADS, FX_DH
    w = fx_w_in[0]
    n = H * dh
    w_a = _pair_heads(w[:, :n], w[:, n:2 * n], H, dh)
    w_main = jnp.concatenate([w_a, w[:, 3 * n:4 * n], _pad_lanes(w[:, 4 * n:])], axis=1)
    a, og, fgate, vt = _proj(
        x2, mod, 1, batch, norm_g[1, 1], w_main.astype(BF16), (2 * n, n, LANES),
        (BF16, BF16, F32), (w[:, 2 * n:3 * n].T.astype(BF16),), (BF16,))
    gqk = jnp.concatenate(
        [fx_qnorm_g[0] * (dh ** -0.5 * LOG2E), fx_knorm_g[0]]).reshape(1, LANES)
    bf_row = _pad_lanes(fx_b_f[0].reshape(1, H))
    q_aug, k_aug, f_ends = _fox_prep(a, fgate, bf_row, gqk, batch)
    qk_bound = (FX_BOUND_SLACK * dh * dh ** -0.5 * LOG2E
                * jnp.max(jnp.abs(fx_qnorm_g[0])) * jnp.max(jnp.abs(fx_knorm_g[0])))
    attn = _fox_attn(q_aug, k_aug, vt, og, f_ends, qk_bound, batch)
    x2 = _ffn(x2, mod, 1, 2, batch, norm_g[1, 2], w_in_b[1, 1], w_out_b[1, 1], final_g=final_g,
              mixer_out=attn, w_mix=fx_w_out[0].astype(BF16))
    return x2.reshape(batch, seq, d)
```

```python
import functools

import jax
import jax.numpy as jnp
from jax import lax
from jax.experimental import pallas as pl
from jax.experimental.pallas import tpu as pltpu

D_MODEL = 1024
D_FF = 4 * D_MODEL
EPS = 1e-6
FFN_RES = 0.5
CONV_W = 4
ML_HEADS = 8
ML_DV = 128
ML_DQK = 64
FX_HEADS = 16
FX_DH = 64
NEG = -1e30

LANES = 128
SUBLANES = 8
VMEM_LIMIT_BYTES = 56 * 1024 * 1024

FFN_TM = 1024
FFN_TF = 1024
PROJ_TM = 512
PROJ_TN = 512
ML_CHUNK = 256
ML_ONES_ROWS = 16
FX_TS = 512
FX_TQ = 512
FX_TK = 512
FX_BIAS_LANE = FX_DH
FX_PRUNE_GAP = 170.0
FX_BOUND_SLACK = 1.05
FX_AHEAD = 1
FX_BUFS = FX_AHEAD + 1
FX_UNROLL = 8
FX_ONES_ROWS = 16
LOG2E = 1.4426950408889634

BF16 = jnp.bfloat16
F32 = jnp.float32


def _cparams(sem):
    return pltpu.CompilerParams(dimension_semantics=sem, vmem_limit_bytes=VMEM_LIMIT_BYTES)


def _dot(a, b):
    return jnp.dot(a, b, preferred_element_type=F32)


def _dot_nt(a, b):
    return lax.dot_general(a, b, (((1,), (1,)), ((), ())), preferred_element_type=F32)


def _dot_tn(a, b):
    return lax.dot_general(a, b, (((0,), (0,)), ((), ())), preferred_element_type=F32)


def _sigmoid(x):
    return 1.0 / (1.0 + jnp.exp(-x))


def _log_sigmoid(x):
    return jnp.minimum(x, 0.0) - jnp.log(1.0 + jnp.exp(-jnp.abs(x)))


def _split3(x):
    hi = x.astype(BF16)
    r1 = x - hi.astype(F32)
    mid = r1.astype(BF16)
    lo = (r1 - mid.astype(F32)).astype(BF16)
    return hi, mid, lo


def _adaln(x, g, shift, scale):
    ms = jnp.mean(x * x, axis=-1, keepdims=True)
    return (x * lax.rsqrt(ms + EPS) * g) * (1.0 + scale) + shift


def _mod_kernel(c_ref, w_ref, b_ref, o_ref):
    c = c_ref[...]
    cond = (c * _sigmoid(c)).astype(BF16)
    o_ref[0] = _dot(cond, w_ref[0].astype(BF16)) + b_ref[0]


def _modulation(c, ada_w, ada_b):
    depth, d, n = ada_w.shape
    b = c.shape[0]
    c_pad = jnp.zeros((SUBLANES, d), F32).at[:b].set(c)
    tn = 1024
    out = pl.pallas_call(
        _mod_kernel,
        grid=(depth, n // tn),
        in_specs=[
            pl.BlockSpec((SUBLANES, d), lambda l, j: (0, 0)),
            pl.BlockSpec((1, d, tn), lambda l, j: (l, 0, j)),
            pl.BlockSpec((1, 1, tn), lambda l, j: (l, 0, j)),
        ],
        out_specs=pl.BlockSpec((1, SUBLANES, tn), lambda l, j: (l, 0, j)),
        out_shape=jax.ShapeDtypeStruct((depth, SUBLANES, n), F32),
        compiler_params=_cparams(("arbitrary", "arbitrary")),
        name="adaln_mod",
    )(c_pad, ada_w, ada_b.reshape(depth, 1, n))
    mod = out[:, :b].reshape(depth, b, 9, d).transpose(0, 2, 1, 3)
    return mod.reshape(depth * 9 * b, 1, d)


def _mod_spec(layer, idx, batch, rows_per_batch_tiles):
    base = (layer * 9 + idx) * batch
    return pl.BlockSpec((None, 1, D_MODEL),
                        lambda m, *_: (base + m // rows_per_batch_tiles, 0, 0))


def _ffn_kernel(*refs, final_norm, mixer):
    refs = list(refs)
    x_ref, sh_ref, sc_ref, gt_ref, g_ref, wg_ref, wu_ref, wo_ref = refs[:8]
    del refs[:8]
    fg_ref = refs.pop(0) if final_norm else None
    if mixer is not None:
        a_ref, wmix_ref, gmix_ref = refs[:3]
        del refs[:3]
        o_ref, h_scr, acc_scr, x1_scr = refs
    else:
        o_ref, h_scr, acc_scr = refs
    f = pl.program_id(1)

    @pl.when(f == 0)
    def _():
        x1 = x_ref[...]
        if mixer == "rows":
            y = _dot(a_ref[...], wmix_ref[...])
        elif mixer == "time_minor":
            y = jnp.concatenate(
                [_dot_tn(a_ref[j], wmix_ref[...]) for j in range(a_ref.shape[0])], axis=0)
        if mixer is not None:
            x1 = x1 + (1.0 + gmix_ref[...]) * y
            x1_scr[...] = x1
        h = _adaln(x1, g_ref[...], sh_ref[...], sc_ref[...])
        h_scr[...] = h.astype(BF16)
        acc_scr[...] = jnp.zeros_like(acc_scr)

    h = h_scr[...]
    gate = _dot(h, wg_ref[...])
    up = _dot(h, wu_ref[...])
    a = (gate * _sigmoid(gate) * up).astype(BF16)
    acc_scr[...] += _dot(a, wo_ref[...])

    @pl.when(f == pl.num_programs(1) - 1)
    def _():
        x1 = x_ref[...] if mixer is None else x1_scr[...]
        out = x1 + (FFN_RES * (1.0 + gt_ref[...])) * acc_scr[...]
        if final_norm:
            ms = jnp.mean(out * out, axis=-1, keepdims=True)
            out = out * lax.rsqrt(ms + EPS) * fg_ref[...]
        o_ref[...] = out


def _ffn(x2, mod, layer, sub, batch, g, w_in, w_out, final_g=None, mixer_out=None, w_mix=None):
    m, d = x2.shape
    f_dim = w_out.shape[0]
    tm, tf = FFN_TM, FFN_TF
    nf = f_dim // tf
    tiles_per_batch = (m // batch) // tm
    final_norm = final_g is not None
    row = lambda m_, f_: (m_, 0)
    const = lambda m_, f_: (0, 0)
    in_specs = [
        pl.BlockSpec((tm, d), row),
        _mod_spec(layer, sub * 3 + 0, batch, tiles_per_batch),
        _mod_spec(layer, sub * 3 + 1, batch, tiles_per_batch),
        _mod_spec(layer, sub * 3 + 2, batch, tiles_per_batch),
        pl.BlockSpec((1, d), const),
        pl.BlockSpec((d, tf), lambda m_, f_: (0, f_)),
        pl.BlockSpec((d, tf), lambda m_, f_: (0, nf + f_)),
        pl.BlockSpec((tf, d), lambda m_, f_: (f_, 0)),
    ]
    args = [x2, mod, mod, mod, g.reshape(1, d), w_in, w_in, w_out]
    scratch = [pltpu.VMEM((tm, d), BF16), pltpu.VMEM((tm, d), F32)]
    if final_norm:
        in_specs.append(pl.BlockSpec((1, d), const))
        args.append(final_g.reshape(1, d))
    mixer = None
    if mixer_out is not None:
        if mixer_out.ndim == 3:
            mixer = "time_minor"
            sub_tiles = tm // PROJ_TM
            in_specs.append(pl.BlockSpec((sub_tiles, mixer_out.shape[1], PROJ_TM),
                                         lambda m_, f_: (m_, 0, 0)))
        else:
            mixer = "rows"
            in_specs.append(pl.BlockSpec((tm, mixer_out.shape[1]), row))
        in_specs += [pl.BlockSpec(w_mix.shape, const),
                     _mod_spec(layer, 5, batch, tiles_per_batch)]
        args += [mixer_out, w_mix, mod]
        scratch.append(pltpu.VMEM((tm, d), F32))
    return pl.pallas_call(
        functools.partial(_ffn_kernel, final_norm=final_norm, mixer=mixer),
        grid=(m // tm, nf),
        in_specs=in_specs,
        out_specs=pl.BlockSpec((tm, d), row),
        out_shape=jax.ShapeDtypeStruct((m, d), F32),
        scratch_shapes=scratch,
        compiler_params=_cparams(("arbitrary", "arbitrary")),
        name="ffn_final" if final_norm else ("ffn" if mixer is None else "ffn_mix"),
    )(*args)


def _proj_kernel(x_ref, sh_ref, sc_ref, g_ref, w_ref, *refs, widths, n_t):
    h = _adaln(x_ref[...], g_ref[...], sh_ref[...], sc_ref[...]).astype(BF16)
    wt_refs = refs[:n_t]
    row_refs = refs[n_t:n_t + len(widths)]
    t_refs = refs[n_t + len(widths):]
    col = 0
    for o_ref, width in zip(row_refs, widths):
        step = min(width, PROJ_TN)
        for c0 in range(0, width, step):
            o_ref[:, c0:c0 + step] = _dot(
                h, w_ref[:, col + c0:col + c0 + step]).astype(o_ref.dtype)
        col += width
    for wt_ref, t_ref in zip(wt_refs, t_refs):
        rows = wt_ref.shape[0]
        step = min(rows, PROJ_TN)
        for r0 in range(0, rows, step):
            t_ref[0, r0:r0 + step, :] = _dot_nt(wt_ref[r0:r0 + step, :], h).astype(t_ref.dtype)


def _proj(x2, mod, layer, batch, g, w_main, widths, row_dtypes, w_ts, t_dtypes):
    m, d = x2.shape
    tm = PROJ_TM
    tiles_per_batch = (m // batch) // tm
    row = lambda m_: (m_, 0)
    const = lambda m_: (0, 0)
    out_shape = [jax.ShapeDtypeStruct((m, w), dt) for w, dt in zip(widths, row_dtypes)]
    out_specs = [pl.BlockSpec((tm, w), row) for w in widths]
    for w_t, dt in zip(w_ts, t_dtypes):
        out_shape.append(jax.ShapeDtypeStruct((m // tm, w_t.shape[0], tm), dt))
        out_specs.append(pl.BlockSpec((1, w_t.shape[0], tm), lambda m_: (m_, 0, 0)))
    return pl.pallas_call(
        functools.partial(_proj_kernel, widths=tuple(widths), n_t=len(w_ts)),
        grid=(m // tm,),
        in_specs=[
            pl.BlockSpec((tm, d), row),
            _mod_spec(layer, 3, batch, tiles_per_batch),
            _mod_spec(layer, 4, batch, tiles_per_batch),
            pl.BlockSpec((1, d), const),
            pl.BlockSpec(w_main.shape, const),
        ] + [pl.BlockSpec(w_t.shape, const) for w_t in w_ts],
        out_specs=out_specs,
        out_shape=out_shape,
        compiler_params=_cparams(("arbitrary",)),
        name="mixer_proj",
    )(x2, mod, mod, g.reshape(1, d), w_main, *w_ts)


def _mlstm_kernel(a_ref, vo_ref, gt_ref, cw_ref, bias_ref, hg_ref, out_ref, xbuf, c_scr, m_scr):
    L = ML_CHUNK
    H = ML_HEADS
    d = a_ref.shape[1]
    c = pl.program_id(1)

    @pl.when(c == 0)
    def _():
        xbuf[0:SUBLANES, :] = jnp.zeros((SUBLANES, xbuf.shape[1]), F32)
        c_scr[...] = jnp.zeros_like(c_scr)
        m_scr[...] = jnp.zeros_like(m_scr)

    xbuf[SUBLANES:SUBLANES + L, :] = a_ref[...].astype(F32)
    y = cw_ref[CONV_W - 1:CONV_W, :] * xbuf[SUBLANES:SUBLANES + L, :]
    for j in range(CONV_W - 1):
        off = SUBLANES - (CONV_W - 1) + j
        y = y + cw_ref[j:j + 1, :] * xbuf[off:off + L, :]
    xbuf[0:SUBLANES, :] = xbuf[L:L + SUBLANES, :]
    act = y * _sigmoid(y)

    gt = gt_ref[0] + bias_ref[...]
    lf_all = _log_sigmoid(gt) * LOG2E
    s_i = lax.broadcasted_iota(jnp.int32, (L, L), 0)
    t_i = lax.broadcasted_iota(jnp.int32, (L, L), 1)
    causal_t = s_i <= t_i
    tup = jnp.where(causal_t, 1.0, 0.0).astype(BF16)
    r_hi, r_mid, r_lo = _split3(lf_all)
    b = (_dot(r_hi, tup) + _dot(r_mid, tup) + _dot(r_lo, tup))[H:2 * H]
    i2 = gt[0:H] * LOG2E
    rowv = i2 - b
    lane = lax.broadcasted_iota(jnp.int32, (H, L), 1)
    cmax = rowv
    shift = 1
    while shift < L:
        cmax = jnp.maximum(cmax, jnp.where(lane >= shift, pltpu.roll(cmax, shift, 1), NEG))
        shift *= 2
    m_prev = m_scr[...]
    u = jnp.maximum(m_prev, cmax)
    w_inter = jnp.exp2(m_prev - u)
    inv_floor = jnp.exp2(-(b + u))
    g_tot = jnp.broadcast_to(b[:, L - 1:L], (H, L))
    a_w = g_tot - b + i2
    m_new = jnp.maximum(g_tot + m_prev, jnp.max(a_w, axis=1, keepdims=True))
    wk = jnp.exp2(a_w - m_new)
    decay = jnp.exp2(g_tot + m_prev - m_new)
    m_scr[...] = m_new

    rv_parts = [p.astype(F32) for p in _split3(rowv)]
    nu_parts = [p.astype(F32) for p in _split3(-u)]
    sub16 = lax.broadcasted_iota(jnp.int32, (2 * SUBLANES, L), 0)
    row16 = lambda x, h_: jnp.broadcast_to(x[h_:h_ + 1, :], (2 * SUBLANES, L))

    lane128 = lax.broadcasted_iota(jnp.int32, (L, LANES), 1)
    first_half = lane128 < ML_DQK
    ones_rows = jnp.ones((ML_ONES_ROWS, L), BF16)

    for h in range(H):
        a_h = act[:, h * LANES:(h + 1) * LANES]
        qb = (jnp.where(first_half, a_h, 0.0) * (ML_DQK ** -0.5)).astype(BF16)
        kb = jnp.where(first_half, pltpu.roll(a_h, ML_DQK, 1), 0.0).astype(BF16)

        src = jnp.where(sub16 == 0, row16(rv_parts[0], h), jnp.where(
            sub16 == 1, row16(rv_parts[1], h), jnp.where(
                sub16 == 2, row16(rv_parts[2], h), jnp.where(sub16 < 6, 1.0, 0.0))))
        dst = jnp.where(sub16 < 3, 1.0, jnp.where(
            sub16 == 3, row16(nu_parts[0], h), jnp.where(
                sub16 == 4, row16(nu_parts[1], h), jnp.where(
                    sub16 == 5, row16(nu_parts[2], h), 0.0))))
        z = _dot_tn(src.astype(BF16), dst.astype(BF16))
        w_intra = jnp.exp2(jnp.where(causal_t, z, NEG))
        s_t = (_dot_nt(kb, qb) * w_intra).astype(BF16)

        va = jnp.concatenate([vo_ref[0, h * ML_DV:(h + 1) * ML_DV, :], ones_rows], axis=0)
        c_prev = c_scr[h]
        num = _dot(va, s_t) + _dot_nt(c_prev.astype(BF16), qb) * w_inter[h:h + 1, :]
        den = num[ML_DV:ML_DV + 1, :]
        h_t = num[:ML_DV, :] / jnp.maximum(jnp.abs(den), inv_floor[h:h + 1, :])

        hn = h_t * lax.rsqrt(jnp.mean(h_t * h_t, axis=0, keepdims=True) + EPS)
        hn = hn * hg_ref[h * ML_DV:(h + 1) * ML_DV, :]
        og = vo_ref[0, d + h * ML_DV:d + (h + 1) * ML_DV, :].astype(F32)
        out_ref[0, h * ML_DV:(h + 1) * ML_DV, :] = (_sigmoid(og) * hn).astype(out_ref.dtype)

        vw = (va.astype(F32) * wk[h:h + 1, :]).astype(BF16)
        c_scr[h] = decay[h:h + 1, 0:LANES] * c_prev + _dot(vw, kb)


def _mlstm(a_pre, vo_t, gates_t, conv_w, bias, hnorm_g, batch):
    m, d = a_pre.shape
    seq = m // batch
    L = ML_CHUNK
    nc = seq // L
    sub = PROJ_TM // L
    const = lambda b, c: (0, 0)
    tmin = lambda b, c: ((b * nc + c) // sub, 0, (b * nc + c) % sub)
    hg = jnp.broadcast_to(hnorm_g.reshape(d, 1), (d, L))
    bias_b = jnp.broadcast_to(bias.reshape(2 * ML_HEADS, 1), (2 * ML_HEADS, L))
    return pl.pallas_call(
        _mlstm_kernel,
        grid=(batch, nc),
        in_specs=[
            pl.BlockSpec((L, d), lambda b, c: (b * nc + c, 0)),
            pl.BlockSpec((1, 2 * d, L), tmin),
            pl.BlockSpec((1, 2 * ML_HEADS, L), tmin),
            pl.BlockSpec(conv_w.shape, const),
            pl.BlockSpec(bias_b.shape, const),
            pl.BlockSpec(hg.shape, const),
        ],
        out_specs=pl.BlockSpec((1, d, L), tmin),
        out_shape=jax.ShapeDtypeStruct((m // PROJ_TM, d, PROJ_TM), BF16),
        scratch_shapes=[
            pltpu.VMEM((SUBLANES + L, d), F32),
            pltpu.VMEM((ML_HEADS, ML_DV + ML_ONES_ROWS, LANES), F32),
            pltpu.VMEM((ML_HEADS, L), F32),
        ],
        compiler_params=_cparams(("arbitrary", "arbitrary")),
        name="mlstm_scan",
    )(a_pre, vo_t, gates_t, conv_w, bias_b, hg)


def _fox_prep_kernel(a_ref, f_ref, bf_ref, gqk_ref, place_ref, q_ref, k_ref, ends_ref, carry):
    ts = FX_TS
    H = FX_HEADS
    t = pl.program_id(1)

    @pl.when(t == 0)
    def _():
        carry[...] = jnp.zeros_like(carry)

    logf = _log_sigmoid(f_ref[...] + bf_ref[...]) * LOG2E
    r_i = lax.broadcasted_iota(jnp.int32, (ts, ts), 0)
    c_i = lax.broadcasted_iota(jnp.int32, (ts, ts), 1)
    tlow = jnp.where(r_i >= c_i, 1.0, 0.0).astype(BF16)
    hi, mid, lo = _split3(logf)
    fcum = _dot(tlow, hi) + _dot(tlow, mid) + _dot(tlow, lo) + carry[0:1, :]
    carry[...] = jnp.broadcast_to(fcum[ts - 1:ts, :], carry.shape)

    sub = lax.broadcasted_iota(jnp.int32, (SUBLANES, LANES), 0)
    ends_ref[0] = jnp.where(
        sub == 0, jnp.broadcast_to(fcum[0:1, :], (SUBLANES, LANES)),
        jnp.broadcast_to(fcum[ts - 1:ts, :], (SUBLANES, LANES)))

    lane = lax.broadcasted_iota(jnp.int32, (ts, LANES), 1)
    f_m = jnp.where(lane < H, fcum, 0.0)
    f_hi = f_m.astype(BF16).astype(F32)
    f_r1 = f_m - f_hi
    f_mid = f_r1.astype(BF16).astype(F32)
    f_lo = f_r1 - f_mid
    f_parts = (f_hi + pltpu.roll(f_mid, H, 1) + pltpu.roll(f_lo, 2 * H, 1)
               + jnp.where(lane == 3 * H, 1.0, 0.0)).astype(BF16)

    first_half = lane < FX_DH
    gqk = gqk_ref[...]
    for h in range(H):
        a = a_ref[:, h * LANES:(h + 1) * LANES].astype(F32)
        lhs = jnp.concatenate([(a * a).astype(BF16), f_parts], axis=1)
        res = _dot(lhs, place_ref[h])
        rinv = lax.rsqrt(res[:, :LANES] * (1.0 / FX_DH) + EPS)
        bias = res[:, LANES:]
        a_n = a * rinv * gqk
        q_ref[:, h * LANES:(h + 1) * LANES] = jnp.where(first_half, a_n, bias).astype(BF16)
        k_ref[:, h * LANES:(h + 1) * LANES] = pltpu.roll(
            jnp.where(first_half, bias, a_n), FX_DH, 1).astype(BF16)


def _fox_placement():
    H = FX_HEADS
    r = lax.broadcasted_iota(jnp.int32, (H, 2 * LANES, 2 * LANES), 1)
    c = lax.broadcasted_iota(jnp.int32, (H, 2 * LANES, 2 * LANES), 2)
    h = lax.broadcasted_iota(jnp.int32, (H, 2 * LANES, 2 * LANES), 0)
    half = (r < LANES) & (c < LANES) & ((r < FX_DH) == (c < FX_DH))
    src = r - LANES
    dst = c - LANES
    qb = FX_BIAS_LANE
    one_src = src == 3 * H
    plus = ((src == h) & (dst == qb)) | ((src == H + h) & (dst == qb + 1)) | (
        (src == 2 * H + h) & (dst == qb + 2)) | (one_src & (dst >= qb + 3) & (dst < qb + 6)) | (
        one_src & (dst >= 0) & (dst < 3))
    minus = ((src == h) & (dst == 3)) | ((src == H + h) & (dst == 4)) | (
        (src == 2 * H + h) & (dst == 5))
    in_place = (r >= LANES) & (c >= LANES)
    val = jnp.where(half | (in_place & plus), 1.0, jnp.where(in_place & minus, -1.0, 0.0))
    return val.astype(BF16)


def _fox_prep(a, fgate, bf_row, gqk, batch):
    m = a.shape[0]
    seq = m // batch
    ts = FX_TS
    nt = seq // ts
    wide = FX_HEADS * LANES
    row = lambda b, t: (b * nt + t, 0)
    const = lambda b, t: (0, 0)
    return pl.pallas_call(
        _fox_prep_kernel,
        grid=(batch, nt),
        in_specs=[
            pl.BlockSpec((ts, wide), row),
            pl.BlockSpec((ts, LANES), row),
            pl.BlockSpec((1, LANES), const),
            pl.BlockSpec((1, LANES), const),
            pl.BlockSpec((FX_HEADS, 2 * LANES, 2 * LANES), lambda b, t: (0, 0, 0)),
        ],
        out_specs=[pl.BlockSpec((ts, wide), row)] * 2
        + [pl.BlockSpec((1, SUBLANES, LANES), lambda b, t: (b * nt + t, 0, 0))],
        out_shape=[jax.ShapeDtypeStruct((m, wide), BF16)] * 2
        + [jax.ShapeDtypeStruct((m // ts, SUBLANES, LANES), F32)],
        scratch_shapes=[pltpu.VMEM((SUBLANES, LANES), F32)],
        compiler_params=_cparams(("arbitrary", "arbitrary")),
        name="fox_prep",
    )(a, fgate, bf_row, gqk, _fox_placement())


def _fox_attn_kernel(qtab_ref, ktab_ref, nit_ref, hperm_ref, qa_ref, qb_ref, ka_ref, kb_ref,
                     va_ref, vb_ref, oga_ref, ogb_ref, o_ref, m_st, acc_st, *score_scr,
                     n_diag, stride):
    tq, tk = FX_TQ, FX_TK
    del hperm_ref
    q_refs, k_refs = (qa_ref, qb_ref), (ka_ref, kb_ref)
    v_refs, og_refs = (va_ref, vb_ref), (oga_ref, ogb_ref)
    step = pl.program_id(0) * pl.num_programs(1) + pl.program_id(1)
    base = step * stride
    bufs = [(score_scr[2 * j], score_scr[2 * j + 1]) for j in range(FX_BUFS)]
    key_i = lax.broadcasted_iota(jnp.int32, (tk, tq), 0)
    qry_i = lax.broadcasted_iota(jnp.int32, (tk, tq), 1)
    causal = key_i <= qry_i
    ones_rows = jnp.ones((FX_ONES_ROWS, tk), BF16)

    def score(e, s_ref, mx_ref):
        q0 = pl.multiple_of(qtab_ref[base + e] * tq, tq)
        k0 = pl.multiple_of(ktab_ref[base + e] * tk, tk)
        for hh in range(2):
            q = q_refs[hh][0, pl.ds(q0, tq), :]
            k = k_refs[hh][0, pl.ds(k0, tk), :]
            s = _dot_nt(k, q)
            s_ref[hh] = s
            mx_ref[hh] = jnp.max(s, axis=0, keepdims=True)

    def values(kj, hh):
        return jnp.concatenate([v_refs[hh][0, kj], ones_rows], axis=0)

    def consume_diag(e, s_ref):
        qi = qtab_ref[base + e]
        for hh in range(2):
            s = jnp.where(causal, s_ref[hh], NEG)
            m_new = jnp.max(s, axis=0, keepdims=True)
            p = jnp.exp2(s - m_new).astype(BF16)
            acc_st[qi, hh] = _dot(values(qi, hh), p)
            m_st[qi, hh] = m_new

    def consume(e, s_ref, mx_ref):
        qi = qtab_ref[base + e]
        kj = ktab_ref[base + e]
        for hh in range(2):
            m_prev = m_st[qi, hh]
            m_new = jnp.maximum(m_prev, mx_ref[hh])
            p = jnp.exp2(s_ref[hh] - m_new).astype(BF16)
            corr = jnp.exp2(m_prev - m_new)
            acc_st[qi, hh] = corr * acc_st[qi, hh] + _dot(values(kj, hh), p)
            m_st[qi, hh] = m_new

    def diag_body(i, carry):
        for j in range(FX_UNROLL):
            e = FX_UNROLL * i + j
            score(e + FX_AHEAD, *bufs[(j + FX_AHEAD) % FX_BUFS])
            consume_diag(e, bufs[j % FX_BUFS][0])
        return carry

    def body(i, carry):
        for j in range(FX_UNROLL):
            e = FX_UNROLL * i + j
            score(e + FX_AHEAD, *bufs[(j + FX_AHEAD) % FX_BUFS])
            consume(e, *bufs[j % FX_BUFS])
        return carry

    for e in range(FX_AHEAD):
        score(e, *bufs[e])
    lax.fori_loop(0, n_diag // FX_UNROLL, diag_body, 0)
    lax.fori_loop(n_diag // FX_UNROLL, n_diag // FX_UNROLL + nit_ref[step], body, 0)

    def finalize(qi, carry):
        for hh in range(2):
            acc = acc_st[qi, hh]
            out = acc[:FX_DH, :] / acc[FX_DH:FX_DH + 1, :]
            og = og_refs[hh][0, qi].astype(F32)
            o_ref[0, qi, hh * FX_DH:(hh + 1) * FX_DH, :] = (_sigmoid(og) * out).astype(o_ref.dtype)
        return carry

    lax.fori_loop(0, n_diag, finalize, 0)


def _fox_block_tables(f_ends, qk_bound, hperm, batch, nq):
    pairs = FX_HEADS // 2
    f_first = f_ends[:, 0, :FX_HEADS].reshape(batch, nq, FX_HEADS)[:, :, hperm]
    f_last = f_ends[:, 1, :FX_HEADS].reshape(batch, nq, FX_HEADS)[:, :, hperm]
    f_first = f_first.reshape(batch, nq, pairs, 2)
    f_last = f_last.reshape(batch, nq, pairs, 2)
    qi = [i for i in range(nq) for _ in range(i)]
    kj = [j for i in range(nq) for j in range(i)]
    gap = f_last[:, jnp.asarray(kj)] - f_first[:, jnp.asarray(qi)]
    dead = jnp.all(gap > 2.0 * qk_bound + FX_PRUNE_GAP, axis=-1)
    dead = dead.transpose(0, 2, 1).astype(jnp.int32)
    full = lambda v: jnp.broadcast_to(jnp.asarray(v, jnp.int32), dead.shape)
    _, q_live_first, k_live_first = lax.sort(
        (dead, full(qi), full(kj)), dimension=-1, is_stable=True, num_keys=1)
    n_live = len(qi) - jnp.sum(dead, axis=-1)
    n_iter = (n_live + FX_UNROLL - 1) // FX_UNROLL
    diag = jnp.broadcast_to(jnp.arange(nq, dtype=jnp.int32), (batch, pairs, nq))
    pad = jnp.zeros((batch, pairs, FX_AHEAD), jnp.int32)
    qtab = jnp.concatenate([diag, q_live_first, pad], axis=-1)
    ktab = jnp.concatenate([diag, k_live_first, pad], axis=-1)
    return qtab.reshape(-1), ktab.reshape(-1), n_iter.reshape(-1), qtab.shape[-1]


def _fox_attn(q_aug, k_aug, vog_t, f_ends, qk_bound, hperm, batch):
    m, wide = q_aug.shape
    seq = m // batch
    tq, tk = FX_TQ, FX_TK
    assert tq == tk == PROJ_TM == FX_TS
    nq = seq // tq
    assert nq % FX_UNROLL == 0 and (nq * (nq - 1) // 2) % FX_UNROLL == 0
    pairs = FX_HEADS // 2
    H = FX_HEADS
    q3 = q_aug.reshape(batch, seq, wide)
    k3 = k_aug.reshape(batch, seq, wide)
    vog4 = vog_t.reshape(batch, nq, 2 * H * FX_DH, tk)
    qtab, ktab, n_iter, stride = _fox_block_tables(f_ends, qk_bound, hperm, batch, nq)
    rows = FX_DH + FX_ONES_ROWS

    def head_spec(shape, which, offset):
        if len(shape) == 3:
            return pl.BlockSpec(shape, lambda b, p, qt, kt, ni, hp: (b, 0, offset + hp[2 * p + which]))
        return pl.BlockSpec(shape, lambda b, p, qt, kt, ni, hp: (b, 0, offset + hp[2 * p + which], 0))

    grid_spec = pltpu.PrefetchScalarGridSpec(
        num_scalar_prefetch=4,
        grid=(batch, pairs),
        in_specs=[
            head_spec((1, seq, LANES), 0, 0), head_spec((1, seq, LANES), 1, 0),
            head_spec((1, seq, LANES), 0, 0), head_spec((1, seq, LANES), 1, 0),
            head_spec((1, nq, FX_DH, tk), 0, 0), head_spec((1, nq, FX_DH, tk), 1, 0),
            head_spec((1, nq, FX_DH, tk), 0, H), head_spec((1, nq, FX_DH, tk), 1, H),
        ],
        out_specs=pl.BlockSpec((1, nq, 2 * FX_DH, tq), lambda b, p, *_: (b, 0, p, 0)),
        scratch_shapes=[pltpu.VMEM((nq, 2, 1, tq), F32),
                        pltpu.VMEM((nq, 2, rows, tq), F32)]
        + [pltpu.VMEM((2, tk, tq), F32), pltpu.VMEM((2, 1, tq), F32)] * FX_BUFS,
    )
    out = pl.pallas_call(
        functools.partial(_fox_attn_kernel, n_diag=nq, stride=stride),
        grid_spec=grid_spec,
        out_shape=jax.ShapeDtypeStruct((batch, nq, H * FX_DH, tq), BF16),
        compiler_params=_cparams(("arbitrary", "arbitrary")),
        name="fox_attn",
    )(qtab, ktab, n_iter, hperm, q3, q3, k3, k3, vog4, vog4, vog4, vog4)
    return out.reshape(batch * nq, H * FX_DH, tq)


def _pair_heads(wq, wk, heads, dh):
    lead = wq.shape[0]
    return jnp.concatenate(
        [wq.reshape(lead, heads, dh), wk.reshape(lead, heads, dh)], axis=2
    ).reshape(lead, heads * 2 * dh)


def _pad_lanes(w):
    return jnp.zeros((w.shape[0], LANES), w.dtype).at[:, :w.shape[1]].set(w)


def kernel(x, c, ada_w, ada_b, norm_g, ffn_w_in, ffn_w_out, ml_w_in, ml_conv_w, ml_b_i, ml_b_f,
           ml_hnorm_g, ml_w_out, fx_w_in, fx_b_f, fx_qnorm_g, fx_knorm_g, fx_w_out, final_g):
    batch, seq, d = x.shape
    assert d == D_MODEL and seq % FFN_TM == 0 and seq % FX_TQ == 0 and seq % ML_CHUNK == 0
    depth = ada_w.shape[0]
    assert depth == 2 and batch <= SUBLANES

    mod = _modulation(c, ada_w, ada_b)
    w_in_b = ffn_w_in.astype(BF16)
    w_out_b = ffn_w_out.astype(BF16)
    x2 = x.reshape(batch * seq, d)

    x2 = _ffn(x2, mod, 0, 0, batch, norm_g[0, 0], w_in_b[0, 0], w_out_b[0, 0])

    H, dk, dv = ML_HEADS, ML_DQK, ML_DV
    w = ml_w_in[0]
    n_qk = H * dk
    w_a = _pair_heads(w[:, :n_qk], w[:, n_qk:2 * n_qk], H, dk)
    w_v = w[:, 2 * n_qk:2 * n_qk + H * dv]
    w_o = w[:, 2 * n_qk + H * dv:2 * n_qk + 2 * H * dv]
    w_g = w[:, 2 * n_qk + 2 * H * dv:]
    w_vo_t = jnp.concatenate([w_v, w_o], axis=1).T.astype(BF16)
    cw = ml_conv_w[0]
    conv_w = _pair_heads(cw[:, :n_qk], cw[:, n_qk:], H, dk)
    bias = jnp.concatenate([ml_b_i[0], ml_b_f[0]])
    a_pre, vo_t, gates_t = _proj(
        x2, mod, 0, batch, norm_g[0, 1], w_a.astype(BF16), (2 * n_qk,), (BF16,),
        (w_vo_t, w_g.T.astype(BF16)), (BF16, F32))
    hs_t = _mlstm(a_pre, vo_t, gates_t, conv_w, bias, ml_hnorm_g[0], batch)
    x2 = _ffn(x2, mod, 0, 2, batch, norm_g[0, 2], w_in_b[0, 1], w_out_b[0, 1],
              mixer_out=hs_t, w_mix=ml_w_out[0].astype(BF16))

    x2 = _ffn(x2, mod, 1, 0, batch, norm_g[1, 0], w_in_b[1, 0], w_out_b[1, 0])

    H, dh = FX_HEADS, FX_DH
    w = fx_w_in[0]
    n = H * dh
    w_a = _pair_heads(w[:, :n], w[:, n:2 * n], H, dh)
    w_main = jnp.concatenate([w_a, _pad_lanes(w[:, 4 * n:])], axis=1)
    a, fgate, vog_t = _proj(
        x2, mod, 1, batch, norm_g[1, 1], w_main.astype(BF16), (2 * n, LANES),
        (BF16, F32), (w[:, 2 * n:4 * n].T.astype(BF16),), (BF16,))
    gqk = jnp.concatenate(
        [fx_qnorm_g[0] * (dh ** -0.5 * LOG2E), fx_knorm_g[0]]).reshape(1, LANES)
    bf_row = _pad_lanes(fx_b_f[0].reshape(1, H))
    q_aug, k_aug, f_ends = _fox_prep(a, fgate, bf_row, gqk, batch)
    qk_bound = (FX_BOUND_SLACK * dh * dh ** -0.5 * LOG2E
                * jnp.max(jnp.abs(fx_qnorm_g[0])) * jnp.max(jnp.abs(fx_knorm_g[0])))
    hperm = jnp.argsort(fx_b_f[0]).astype(jnp.int32)
    attn_t = _fox_attn(q_aug, k_aug, vog_t, f_ends, qk_bound, hperm, batch)
    w_mix = fx_w_out[0].reshape(H, dh, d)[hperm].reshape(H * dh, d).astype(BF16)
    x2 = _ffn(x2, mod, 1, 2, batch, norm_g[1, 2], w_in_b[1, 1], w_out_b[1, 1], final_g=final_g,
              mixer_out=attn_t, w_mix=w_mix)
    return x2.reshape(batch, seq, d)
```

```python
import functools

import jax
import jax.numpy as jnp
from jax import lax
from jax.experimental import pallas as pl
from jax.experimental.pallas import tpu as pltpu

D_MODEL = 1024
N_MOD = 9
EPS = 1e-6
FFN_RES = 0.5
CONV_W = 4
ML_HEADS = 8
ML_DV = 128
ML_DQK = 64
FX_HEADS = 16
FX_DH = 64
NEG = -1e30

LANES = 128
SUBLANES = 8
VMEM_LIMIT_BYTES = 56 * 1024 * 1024

MOD_TN = 1024
FFN_TM = 1024
FFN_TF = 1024
PROJ_TM = 512
PROJ_TN = 512
ML_CHUNK = 256
ML_HEAD_GROUP = 4
ML_ONES_ROWS = 16
FX_TS = 512
FX_TQ = 512
FX_TK = 512
FX_BIAS_LANE = FX_DH
FX_PRUNE_GAP = 170.0
FX_BOUND_SLACK = 1.05
FX_AHEAD = 1
FX_BUFS = FX_AHEAD + 1
FX_UNROLL = 8
FX_ONES_ROWS = 16
LOG2E = 1.4426950408889634

BF16 = jnp.bfloat16
F32 = jnp.float32


def _cparams(sem):
    return pltpu.CompilerParams(dimension_semantics=sem, vmem_limit_bytes=VMEM_LIMIT_BYTES)


def _dot(a, b):
    return jnp.dot(a, b, preferred_element_type=F32)


def _dot_nt(a, b):
    return lax.dot_general(a, b, (((1,), (1,)), ((), ())), preferred_element_type=F32)


def _dot_tn(a, b):
    return lax.dot_general(a, b, (((0,), (0,)), ((), ())), preferred_element_type=F32)


def _sigmoid(x):
    return 1.0 / (1.0 + jnp.exp(-x))


def _log_sigmoid(x):
    return jnp.minimum(x, 0.0) - jnp.log(1.0 + jnp.exp(-jnp.abs(x)))


def _split3(x):
    hi = x.astype(BF16)
    r1 = x - hi.astype(F32)
    mid = r1.astype(BF16)
    lo = (r1 - mid.astype(F32)).astype(BF16)
    return hi, mid, lo


def _adaln(x, g, shift, scale):
    ms = jnp.mean(x * x, axis=-1, keepdims=True)
    return (x * lax.rsqrt(ms + EPS)) * (g * (1.0 + scale)) + shift


def _mod_kernel(c_ref, w_ref, b_ref, o_ref):
    c = c_ref[...]
    cond = (c * _sigmoid(c)).astype(BF16)
    o_ref[0] = _dot(cond, w_ref[0].astype(BF16)) + b_ref[0]


def _modulation(c, ada_w, ada_b):
    depth, d, n = ada_w.shape
    b = c.shape[0]
    c_pad = jnp.zeros((SUBLANES, d), F32).at[:b].set(c)
    tn = MOD_TN
    out = pl.pallas_call(
        _mod_kernel,
        grid=(depth, n // tn),
        in_specs=[
            pl.BlockSpec((SUBLANES, d), lambda l, j: (0, 0)),
            pl.BlockSpec((1, d, tn), lambda l, j: (l, 0, j)),
            pl.BlockSpec((1, 1, tn), lambda l, j: (l, 0, j)),
        ],
        out_specs=pl.BlockSpec((1, SUBLANES, tn), lambda l, j: (l, 0, j)),
        out_shape=jax.ShapeDtypeStruct((depth, SUBLANES, n), F32),
        compiler_params=_cparams(("arbitrary", "arbitrary")),
        name="adaln_mod",
    )(c_pad, ada_w, ada_b.reshape(depth, 1, n))
    mod = out[:, :b].reshape(depth, b, N_MOD, d).transpose(0, 2, 1, 3)
    return mod.reshape(depth * N_MOD * b, 1, d)


def _mod_spec(layer, idx, batch, rows_per_batch_tiles):
    base = (layer * N_MOD + idx) * batch
    return pl.BlockSpec((None, 1, D_MODEL),
                        lambda m, *_: (base + m // rows_per_batch_tiles, 0, 0))


def _ffn_kernel(*refs, final_norm, mixer):
    refs = list(refs)
    x_ref, sh_ref, sc_ref, gt_ref, g_ref, wg_ref, wu_ref, wo_ref = refs[:8]
    del refs[:8]
    fg_ref = refs.pop(0) if final_norm else None
    if mixer is not None:
        a_ref, wmix_ref, gmix_ref = refs[:3]
        del refs[:3]
        o_ref, h_scr, acc_scr, x1_scr = refs
    else:
        o_ref, h_scr, acc_scr = refs
    f = pl.program_id(1)

    @pl.when(f == 0)
    def _():
        x1 = x_ref[...]
        if mixer == "rows":
            y = _dot(a_ref[...], wmix_ref[...])
        elif mixer == "time_minor":
            y = jnp.concatenate(
                [_dot_tn(a_ref[j], wmix_ref[...]) for j in range(a_ref.shape[0])], axis=0)
        if mixer is not None:
            x1 = x1 + (1.0 + gmix_ref[...]) * y
            x1_scr[...] = x1
        h = _adaln(x1, g_ref[...], sh_ref[...], sc_ref[...])
        h_scr[...] = h.astype(BF16)
        acc_scr[...] = jnp.zeros_like(acc_scr)

    h = h_scr[...]
    gate = _dot(h, wg_ref[...])
    up = _dot(h, wu_ref[...])
    a = (gate * _sigmoid(gate) * up).astype(BF16)
    acc_scr[...] += _dot(a, wo_ref[...])

    @pl.when(f == pl.num_programs(1) - 1)
    def _():
        x1 = x_ref[...] if mixer is None else x1_scr[...]
        out = x1 + (FFN_RES * (1.0 + gt_ref[...])) * acc_scr[...]
        if final_norm:
            ms = jnp.mean(out * out, axis=-1, keepdims=True)
            out = out * lax.rsqrt(ms + EPS) * fg_ref[...]
        o_ref[...] = out


def _ffn(x2, mod, layer, sub, batch, g, w_in, w_out, final_g=None, mixer_out=None, w_mix=None):
    m, d = x2.shape
    f_dim = w_out.shape[0]
    tm, tf = FFN_TM, FFN_TF
    nf = f_dim // tf
    tiles_per_batch = (m // batch) // tm
    final_norm = final_g is not None
    row = lambda m_, f_: (m_, 0)
    const = lambda m_, f_: (0, 0)
    in_specs = [
        pl.BlockSpec((tm, d), row),
        _mod_spec(layer, sub * 3 + 0, batch, tiles_per_batch),
        _mod_spec(layer, sub * 3 + 1, batch, tiles_per_batch),
        _mod_spec(layer, sub * 3 + 2, batch, tiles_per_batch),
        pl.BlockSpec((1, d), const),
        pl.BlockSpec((d, tf), lambda m_, f_: (0, f_)),
        pl.BlockSpec((d, tf), lambda m_, f_: (0, nf + f_)),
        pl.BlockSpec((tf, d), lambda m_, f_: (f_, 0)),
    ]
    args = [x2, mod, mod, mod, g.reshape(1, d), w_in, w_in, w_out]
    scratch = [pltpu.VMEM((tm, d), BF16), pltpu.VMEM((tm, d), F32)]
    if final_norm:
        in_specs.append(pl.BlockSpec((1, d), const))
        args.append(final_g.reshape(1, d))
    mixer = None
    if mixer_out is not None:
        if mixer_out.ndim == 3:
            mixer = "time_minor"
            sub_tiles = tm // PROJ_TM
            in_specs.append(pl.BlockSpec((sub_tiles, mixer_out.shape[1], PROJ_TM),
                                         lambda m_, f_: (m_, 0, 0)))
        else:
            mixer = "rows"
            in_specs.append(pl.BlockSpec((tm, mixer_out.shape[1]), row))
        in_specs += [pl.BlockSpec(w_mix.shape, const),
                     _mod_spec(layer, 5, batch, tiles_per_batch)]
        args += [mixer_out, w_mix, mod]
        scratch.append(pltpu.VMEM((tm, d), F32))
    return pl.pallas_call(
        functools.partial(_ffn_kernel, final_norm=final_norm, mixer=mixer),
        grid=(m // tm, nf),
        in_specs=in_specs,
        out_specs=pl.BlockSpec((tm, d), row),
        out_shape=jax.ShapeDtypeStruct((m, d), F32),
        scratch_shapes=scratch,
        compiler_params=_cparams(("arbitrary", "arbitrary")),
        name="ffn_final" if final_norm else ("ffn" if mixer is None else "ffn_mix"),
    )(*args)


def _proj_kernel(x_ref, sh_ref, sc_ref, g_ref, w_ref, *refs, widths, n_t):
    h = _adaln(x_ref[...], g_ref[...], sh_ref[...], sc_ref[...]).astype(BF16)
    wt_refs = refs[:n_t]
    row_refs = refs[n_t:n_t + len(widths)]
    t_refs = refs[n_t + len(widths):]
    col = 0
    for o_ref, width in zip(row_refs, widths):
        step = min(width, PROJ_TN)
        for c0 in range(0, width, step):
            o_ref[:, c0:c0 + step] = _dot(
                h, w_ref[:, col + c0:col + c0 + step]).astype(o_ref.dtype)
        col += width
    for wt_ref, t_ref in zip(wt_refs, t_refs):
        rows = wt_ref.shape[0]
        step = min(rows, PROJ_TN)
        for r0 in range(0, rows, step):
            t_ref[0, r0:r0 + step, :] = _dot_nt(wt_ref[r0:r0 + step, :], h).astype(t_ref.dtype)


def _proj(x2, mod, layer, batch, g, w_main, widths, row_dtypes, w_ts, t_dtypes):
    m, d = x2.shape
    tm = PROJ_TM
    tiles_per_batch = (m // batch) // tm
    row = lambda m_: (m_, 0)
    const = lambda m_: (0, 0)
    out_shape = [jax.ShapeDtypeStruct((m, w), dt) for w, dt in zip(widths, row_dtypes)]
    out_specs = [pl.BlockSpec((tm, w), row) for w in widths]
    for w_t, dt in zip(w_ts, t_dtypes):
        out_shape.append(jax.ShapeDtypeStruct((m // tm, w_t.shape[0], tm), dt))
        out_specs.append(pl.BlockSpec((1, w_t.shape[0], tm), lambda m_: (m_, 0, 0)))
    return pl.pallas_call(
        functools.partial(_proj_kernel, widths=tuple(widths), n_t=len(w_ts)),
        grid=(m // tm,),
        in_specs=[
            pl.BlockSpec((tm, d), row),
            _mod_spec(layer, 3, batch, tiles_per_batch),
            _mod_spec(layer, 4, batch, tiles_per_batch),
            pl.BlockSpec((1, d), const),
            pl.BlockSpec(w_main.shape, const),
        ] + [pl.BlockSpec(w_t.shape, const) for w_t in w_ts],
        out_specs=out_specs,
        out_shape=out_shape,
        compiler_params=_cparams(("arbitrary",)),
        name="mixer_proj",
    )(x2, mod, mod, g.reshape(1, d), w_main, *w_ts)


def _mlstm_kernel(a_ref, vo_ref, gt_ref, cw_ref, bias_ref, hg_ref, out_ref, xbuf, c_scr, m_scr):
    L = ML_CHUNK
    H = ML_HEADS
    d = a_ref.shape[1]
    c = pl.program_id(1)

    @pl.when(c == 0)
    def _():
        xbuf[0:SUBLANES, :] = jnp.zeros((SUBLANES, xbuf.shape[1]), F32)
        c_scr[...] = jnp.zeros_like(c_scr)
        m_scr[...] = jnp.zeros_like(m_scr)

    xbuf[SUBLANES:SUBLANES + L, :] = a_ref[...].astype(F32)
    y = cw_ref[CONV_W - 1:CONV_W, :] * xbuf[SUBLANES:SUBLANES + L, :]
    for j in range(CONV_W - 1):
        off = SUBLANES - (CONV_W - 1) + j
        y = y + cw_ref[j:j + 1, :] * xbuf[off:off + L, :]
    xbuf[0:SUBLANES, :] = xbuf[L:L + SUBLANES, :]
    act = y * _sigmoid(y)

    gt = gt_ref[0] + bias_ref[...]
    lf_all = _log_sigmoid(gt) * LOG2E
    s_i = lax.broadcasted_iota(jnp.int32, (L, L), 0)
    t_i = lax.broadcasted_iota(jnp.int32, (L, L), 1)
    causal_t = s_i <= t_i
    tup = jnp.where(causal_t, 1.0, 0.0).astype(BF16)
    r_hi, r_mid, r_lo = _split3(lf_all)
    b = (_dot(r_hi, tup) + _dot(r_mid, tup) + _dot(r_lo, tup))[H:2 * H]
    i2 = gt[0:H] * LOG2E
    rowv = i2 - b
    lane = lax.broadcasted_iota(jnp.int32, (H, L), 1)
    cmax = rowv
    shift = 1
    while shift < L:
        cmax = jnp.maximum(cmax, jnp.where(lane >= shift, pltpu.roll(cmax, shift, 1), NEG))
        shift *= 2
    m_prev = m_scr[...]
    u = jnp.maximum(m_prev, cmax)
    w_inter = jnp.exp2(m_prev - u)
    inv_floor = jnp.exp2(-(b + u))
    g_tot = jnp.broadcast_to(b[:, L - 1:L], (H, L))
    a_w = g_tot - b + i2
    m_new = jnp.maximum(g_tot + m_prev, jnp.max(a_w, axis=1, keepdims=True))
    wk = jnp.exp2(a_w - m_new)
    decay = jnp.exp2(g_tot + m_prev - m_new)
    m_scr[...] = m_new

    rv_parts = [p.astype(F32) for p in _split3(rowv)]
    nu_parts = [p.astype(F32) for p in _split3(-u)]
    sub16 = lax.broadcasted_iota(jnp.int32, (2 * SUBLANES, L), 0)
    row16 = lambda x, h_: jnp.broadcast_to(x[h_:h_ + 1, :], (2 * SUBLANES, L))

    lane128 = lax.broadcasted_iota(jnp.int32, (L, LANES), 1)
    first_half = lane128 < ML_DQK
    ones_rows = jnp.ones((ML_ONES_ROWS, L), BF16)

    for h0 in range(0, H, ML_HEAD_GROUP):
        group = range(h0, h0 + ML_HEAD_GROUP)
        qbs, kbs, zs, s_ts, vas, nums = {}, {}, {}, {}, {}, {}
        for h in group:
            a_h = act[:, h * LANES:(h + 1) * LANES]
            qbs[h] = (jnp.where(first_half, a_h, 0.0) * (ML_DQK ** -0.5)).astype(BF16)
            kbs[h] = jnp.where(first_half, pltpu.roll(a_h, ML_DQK, 1), 0.0).astype(BF16)
        for h in group:
            src = jnp.where(sub16 == 0, row16(rv_parts[0], h), jnp.where(
                sub16 == 1, row16(rv_parts[1], h), jnp.where(
                    sub16 == 2, row16(rv_parts[2], h), jnp.where(sub16 < 6, 1.0, 0.0))))
            dst = jnp.where(sub16 < 3, 1.0, jnp.where(
                sub16 == 3, row16(nu_parts[0], h), jnp.where(
                    sub16 == 4, row16(nu_parts[1], h), jnp.where(
                        sub16 == 5, row16(nu_parts[2], h), 0.0))))
            zs[h] = _dot_tn(src.astype(BF16), dst.astype(BF16))
        for h in group:
            w_intra = jnp.exp2(jnp.where(causal_t, zs[h], NEG))
            s_ts[h] = (_dot_nt(kbs[h], qbs[h]) * w_intra).astype(BF16)
        for h in group:
            vas[h] = jnp.concatenate(
                [vo_ref[0, h * ML_DV:(h + 1) * ML_DV, :], ones_rows], axis=0)
            nums[h] = (_dot(vas[h], s_ts[h])
                       + _dot_nt(c_scr[h].astype(BF16), qbs[h]) * w_inter[h:h + 1, :])
        for h in group:
            num = nums[h]
            den = num[ML_DV:ML_DV + 1, :]
            h_t = num[:ML_DV, :] / jnp.maximum(jnp.abs(den), inv_floor[h:h + 1, :])
            hn = h_t * lax.rsqrt(jnp.mean(h_t * h_t, axis=0, keepdims=True) + EPS)
            hn = hn * hg_ref[h * ML_DV:(h + 1) * ML_DV, :]
            og = vo_ref[0, d + h * ML_DV:d + (h + 1) * ML_DV, :].astype(F32)
            out_ref[0, h * ML_DV:(h + 1) * ML_DV, :] = (_sigmoid(og) * hn).astype(out_ref.dtype)
        for h in group:
            vw = (vas[h].astype(F32) * wk[h:h + 1, :]).astype(BF16)
            c_scr[h] = decay[h:h + 1, 0:LANES] * c_scr[h] + _dot(vw, kbs[h])


def _mlstm(a_pre, vo_t, gates_t, conv_w, bias, hnorm_g, batch):
    m, d = a_pre.shape
    seq = m // batch
    L = ML_CHUNK
    nc = seq // L
    sub = PROJ_TM // L
    const = lambda b, c: (0, 0)
    tmin = lambda b, c: ((b * nc + c) // sub, 0, (b * nc + c) % sub)
    hg = jnp.broadcast_to(hnorm_g.reshape(d, 1), (d, L))
    bias_b = jnp.broadcast_to(bias.reshape(2 * ML_HEADS, 1), (2 * ML_HEADS, L))
    return pl.pallas_call(
        _mlstm_kernel,
        grid=(batch, nc),
        in_specs=[
            pl.BlockSpec((L, d), lambda b, c: (b * nc + c, 0)),
            pl.BlockSpec((1, 2 * d, L), tmin),
            pl.BlockSpec((1, 2 * ML_HEADS, L), tmin),
            pl.BlockSpec(conv_w.shape, const),
            pl.BlockSpec(bias_b.shape, const),
            pl.BlockSpec(hg.shape, const),
        ],
        out_specs=pl.BlockSpec((1, d, L), tmin),
        out_shape=jax.ShapeDtypeStruct((m // PROJ_TM, d, PROJ_TM), BF16),
        scratch_shapes=[
            pltpu.VMEM((SUBLANES + L, d), F32),
            pltpu.VMEM((ML_HEADS, ML_DV + ML_ONES_ROWS, LANES), F32),
            pltpu.VMEM((ML_HEADS, L), F32),
        ],
        compiler_params=_cparams(("arbitrary", "arbitrary")),
        name="mlstm_scan",
    )(a_pre, vo_t, gates_t, conv_w, bias_b, hg)


def _fox_prep_kernel(a_ref, f_ref, bf_ref, gqk_ref, place_ref, q_ref, k_ref, ends_ref, carry):
    ts = FX_TS
    H = FX_HEADS
    t = pl.program_id(1)

    @pl.when(t == 0)
    def _():
        carry[...] = jnp.zeros_like(carry)

    logf = _log_sigmoid(f_ref[...] + bf_ref[...]) * LOG2E
    r_i = lax.broadcasted_iota(jnp.int32, (ts, ts), 0)
    c_i = lax.broadcasted_iota(jnp.int32, (ts, ts), 1)
    tlow = jnp.where(r_i >= c_i, 1.0, 0.0).astype(BF16)
    hi, mid, lo = _split3(logf)
    fcum = _dot(tlow, hi) + _dot(tlow, mid) + _dot(tlow, lo) + carry[0:1, :]
    carry[...] = jnp.broadcast_to(fcum[ts - 1:ts, :], carry.shape)

    sub = lax.broadcasted_iota(jnp.int32, (SUBLANES, LANES), 0)
    ends_ref[0] = jnp.where(
        sub == 0, jnp.broadcast_to(fcum[0:1, :], (SUBLANES, LANES)),
        jnp.broadcast_to(fcum[ts - 1:ts, :], (SUBLANES, LANES)))

    lane = lax.broadcasted_iota(jnp.int32, (ts, LANES), 1)
    f_m = jnp.where(lane < H, fcum, 0.0)
    f_hi = f_m.astype(BF16).astype(F32)
    f_r1 = f_m - f_hi
    f_mid = f_r1.astype(BF16).astype(F32)
    f_lo = f_r1 - f_mid
    f_parts = (f_hi + pltpu.roll(f_mid, H, 1) + pltpu.roll(f_lo, 2 * H, 1)
               + jnp.where(lane == 3 * H, 1.0, 0.0)).astype(BF16)

    first_half = lane < FX_DH
    gqk = gqk_ref[...]
    for h in range(H):
        a = a_ref[:, h * LANES:(h + 1) * LANES].astype(F32)
        lhs = jnp.concatenate([(a * a).astype(BF16), f_parts], axis=1)
        res = _dot(lhs, place_ref[h])
        rinv = lax.rsqrt(res[:, :LANES] * (1.0 / FX_DH) + EPS)
        bias = res[:, LANES:]
        a_n = a * rinv * gqk
        q_ref[:, h * LANES:(h + 1) * LANES] = jnp.where(first_half, a_n, bias).astype(BF16)
        k_ref[:, h * LANES:(h + 1) * LANES] = pltpu.roll(
            jnp.where(first_half, bias, a_n), FX_DH, 1).astype(BF16)


def _fox_placement():
    H = FX_HEADS
    r = lax.broadcasted_iota(jnp.int32, (H, 2 * LANES, 2 * LANES), 1)
    c = lax.broadcasted_iota(jnp.int32, (H, 2 * LANES, 2 * LANES), 2)
    h = lax.broadcasted_iota(jnp.int32, (H, 2 * LANES, 2 * LANES), 0)
    half = (r < LANES) & (c < LANES) & ((r < FX_DH) == (c < FX_DH))
    src = r - LANES
    dst = c - LANES
    qb = FX_BIAS_LANE
    one_src = src == 3 * H
    plus = ((src == h) & (dst == qb)) | ((src == H + h) & (dst == qb + 1)) | (
        (src == 2 * H + h) & (dst == qb + 2)) | (one_src & (dst >= qb + 3) & (dst < qb + 6)) | (
        one_src & (dst >= 0) & (dst < 3))
    minus = ((src == h) & (dst == 3)) | ((src == H + h) & (dst == 4)) | (
        (src == 2 * H + h) & (dst == 5))
    in_place = (r >= LANES) & (c >= LANES)
    val = jnp.where(half | (in_place & plus), 1.0, jnp.where(in_place & minus, -1.0, 0.0))
    return val.astype(BF16)


def _fox_prep(a, fgate, bf_row, gqk, batch):
    m = a.shape[0]
    seq = m // batch
    ts = FX_TS
    nt = seq // ts
    wide = FX_HEADS * LANES
    row = lambda b, t: (b * nt + t, 0)
    const = lambda b, t: (0, 0)
    return pl.pallas_call(
        _fox_prep_kernel,
        grid=(batch, nt),
        in_specs=[
            pl.BlockSpec((ts, wide), row),
            pl.BlockSpec((ts, LANES), row),
            pl.BlockSpec((1, LANES), const),
            pl.BlockSpec((1, LANES), const),
            pl.BlockSpec((FX_HEADS, 2 * LANES, 2 * LANES), lambda b, t: (0, 0, 0)),
        ],
        out_specs=[pl.BlockSpec((ts, wide), row)] * 2
        + [pl.BlockSpec((1, SUBLANES, LANES), lambda b, t: (b * nt + t, 0, 0))],
        out_shape=[jax.ShapeDtypeStruct((m, wide), BF16)] * 2
        + [jax.ShapeDtypeStruct((m // ts, SUBLANES, LANES), F32)],
        scratch_shapes=[pltpu.VMEM((SUBLANES, LANES), F32)],
        compiler_params=_cparams(("arbitrary", "arbitrary")),
        name="fox_prep",
    )(a, fgate, bf_row, gqk, _fox_placement())


def _fox_attn_kernel(qtab_ref, ktab_ref, nit_ref, hperm_ref, qa_ref, qb_ref, ka_ref, kb_ref,
                     va_ref, vb_ref, oga_ref, ogb_ref, o_ref, m_st, acc_st, *score_scr,
                     n_diag, stride):
    tq, tk = FX_TQ, FX_TK
    del hperm_ref
    q_refs, k_refs = (qa_ref, qb_ref), (ka_ref, kb_ref)
    v_refs, og_refs = (va_ref, vb_ref), (oga_ref, ogb_ref)
    step = pl.program_id(0) * pl.num_programs(1) + pl.program_id(1)
    base = step * stride
    bufs = [(score_scr[2 * j], score_scr[2 * j + 1]) for j in range(FX_BUFS)]
    key_i = lax.broadcasted_iota(jnp.int32, (tk, tq), 0)
    qry_i = lax.broadcasted_iota(jnp.int32, (tk, tq), 1)
    causal = key_i <= qry_i
    ones_rows = jnp.ones((FX_ONES_ROWS, tk), BF16)

    def score(e, s_ref, mx_ref):
        q0 = pl.multiple_of(qtab_ref[base + e] * tq, tq)
        k0 = pl.multiple_of(ktab_ref[base + e] * tk, tk)
        for hh in range(2):
            q = q_refs[hh][0, pl.ds(q0, tq), :]
            k = k_refs[hh][0, pl.ds(k0, tk), :]
            s = _dot_nt(k, q)
            s_ref[hh] = s
            mx_ref[hh] = jnp.max(s, axis=0, keepdims=True)

    def values(kj, hh):
        return jnp.concatenate([v_refs[hh][0, kj], ones_rows], axis=0)

    def consume_diag(e, s_ref):
        qi = qtab_ref[base + e]
        for hh in range(2):
            s = jnp.where(causal, s_ref[hh], NEG)
            m_new = jnp.max(s, axis=0, keepdims=True)
            p = jnp.exp2(s - m_new).astype(BF16)
            acc_st[qi, hh] = _dot(values(qi, hh), p)
            m_st[qi, hh] = m_new

    def consume(e, s_ref, mx_ref):
        qi = qtab_ref[base + e]
        kj = ktab_ref[base + e]
        for hh in range(2):
            m_prev = m_st[qi, hh]
            m_new = jnp.maximum(m_prev, mx_ref[hh])
            p = jnp.exp2(s_ref[hh] - m_new).astype(BF16)
            corr = jnp.exp2(m_prev - m_new)
            acc_st[qi, hh] = corr * acc_st[qi, hh] + _dot(values(kj, hh), p)
            m_st[qi, hh] = m_new

    def diag_body(i, carry):
        for j in range(FX_UNROLL):
            e = FX_UNROLL * i + j
            score(e + FX_AHEAD, *bufs[(j + FX_AHEAD) % FX_BUFS])
            consume_diag(e, bufs[j % FX_BUFS][0])
        return carry

    def body(i, carry):
        for j in range(FX_UNROLL):
            e = FX_UNROLL * i + j
            score(e + FX_AHEAD, *bufs[(j + FX_AHEAD) % FX_BUFS])
            consume(e, *bufs[j % FX_BUFS])
        return carry

    for e in range(FX_AHEAD):
        score(e, *bufs[e])
    lax.fori_loop(0, n_diag // FX_UNROLL, diag_body, 0)
    lax.fori_loop(n_diag // FX_UNROLL, n_diag // FX_UNROLL + nit_ref[step], body, 0)

    def finalize(qi, carry):
        for hh in range(2):
            acc = acc_st[qi, hh]
            out = acc[:FX_DH, :] / acc[FX_DH:FX_DH + 1, :]
            og = og_refs[hh][0, qi].astype(F32)
            o_ref[0, qi, hh * FX_DH:(hh + 1) * FX_DH, :] = (_sigmoid(og) * out).astype(o_ref.dtype)
        return carry

    lax.fori_loop(0, n_diag, finalize, 0)


def _fox_block_tables(f_ends, qk_bound, hperm, batch, nq):
    pairs = FX_HEADS // 2
    f_first = f_ends[:, 0, :FX_HEADS].reshape(batch, nq, FX_HEADS)[:, :, hperm]
    f_last = f_ends[:, 1, :FX_HEADS].reshape(batch, nq, FX_HEADS)[:, :, hperm]
    f_first = f_first.reshape(batch, nq, pairs, 2)
    f_last = f_last.reshape(batch, nq, pairs, 2)
    qi = [i for i in range(nq) for _ in range(i)]
    kj = [j for i in range(nq) for j in range(i)]
    gap = f_last[:, jnp.asarray(kj)] - f_first[:, jnp.asarray(qi)]
    dead = jnp.all(gap > 2.0 * qk_bound + FX_PRUNE_GAP, axis=-1)
    dead = dead.transpose(0, 2, 1).astype(jnp.int32)
    full = lambda v: jnp.broadcast_to(jnp.asarray(v, jnp.int32), dead.shape)
    _, q_live_first, k_live_first = lax.sort(
        (dead, full(qi), full(kj)), dimension=-1, is_stable=True, num_keys=1)
    n_live = len(qi) - jnp.sum(dead, axis=-1)
    n_iter = (n_live + FX_UNROLL - 1) // FX_UNROLL
    diag = jnp.broadcast_to(jnp.arange(nq, dtype=jnp.int32), (batch, pairs, nq))
    pad = jnp.zeros((batch, pairs, FX_AHEAD), jnp.int32)
    qtab = jnp.concatenate([diag, q_live_first, pad], axis=-1)
    ktab = jnp.concatenate([diag, k_live_first, pad], axis=-1)
    return qtab.reshape(-1), ktab.reshape(-1), n_iter.reshape(-1), qtab.shape[-1]


def _fox_attn(q_aug, k_aug, vog_t, f_ends, qk_bound, hperm, batch):
    m, wide = q_aug.shape
    seq = m // batch
    tq, tk = FX_TQ, FX_TK
    assert tq == tk == PROJ_TM == FX_TS
    nq = seq // tq
    assert nq % FX_UNROLL == 0 and (nq * (nq - 1) // 2) % FX_UNROLL == 0
    pairs = FX_HEADS // 2
    H = FX_HEADS
    q3 = q_aug.reshape(batch, seq, wide)
    k3 = k_aug.reshape(batch, seq, wide)
    vog4 = vog_t.reshape(batch, nq, 2 * H * FX_DH, tk)
    qtab, ktab, n_iter, stride = _fox_block_tables(f_ends, qk_bound, hperm, batch, nq)
    rows = FX_DH + FX_ONES_ROWS

    def head_spec(shape, which, offset):
        if len(shape) == 3:
            return pl.BlockSpec(shape, lambda b, p, qt, kt, ni, hp: (b, 0, offset + hp[2 * p + which]))
        return pl.BlockSpec(shape, lambda b, p, qt, kt, ni, hp: (b, 0, offset + hp[2 * p + which], 0))

    grid_spec = pltpu.PrefetchScalarGridSpec(
        num_scalar_prefetch=4,
        grid=(batch, pairs),
        in_specs=[
            head_spec((1, seq, LANES), 0, 0), head_spec((1, seq, LANES), 1, 0),
            head_spec((1, seq, LANES), 0, 0), head_spec((1, seq, LANES), 1, 0),
            head_spec((1, nq, FX_DH, tk), 0, 0), head_spec((1, nq, FX_DH, tk), 1, 0),
            head_spec((1, nq, FX_DH, tk), 0, H), head_spec((1, nq, FX_DH, tk), 1, H),
        ],
        out_specs=pl.BlockSpec((1, nq, 2 * FX_DH, tq), lambda b, p, *_: (b, 0, p, 0)),
        scratch_shapes=[pltpu.VMEM((nq, 2, 1, tq), F32),
                        pltpu.VMEM((nq, 2, rows, tq), F32)]
        + [pltpu.VMEM((2, tk, tq), F32), pltpu.VMEM((2, 1, tq), F32)] * FX_BUFS,
    )
    out = pl.pallas_call(
        functools.partial(_fox_attn_kernel, n_diag=nq, stride=stride),
        grid_spec=grid_spec,
        out_shape=jax.ShapeDtypeStruct((batch, nq, H * FX_DH, tq), BF16),
        compiler_params=_cparams(("arbitrary", "arbitrary")),
        name="fox_attn",
    )(qtab, ktab, n_iter, hperm, q3, q3, k3, k3, vog4, vog4, vog4, vog4)
    return out.reshape(batch * nq, H * FX_DH, tq)


def _pair_heads(wq, wk, heads, dh):
    lead = wq.shape[0]
    return jnp.concatenate(
        [wq.reshape(lead, heads, dh), wk.reshape(lead, heads, dh)], axis=2
    ).reshape(lead, heads * 2 * dh)


def _pad_lanes(w):
    return jnp.zeros((w.shape[0], LANES), w.dtype).at[:, :w.shape[1]].set(w)


def kernel(x, c, ada_w, ada_b, norm_g, ffn_w_in, ffn_w_out, ml_w_in, ml_conv_w, ml_b_i, ml_b_f,
           ml_hnorm_g, ml_w_out, fx_w_in, fx_b_f, fx_qnorm_g, fx_knorm_g, fx_w_out, final_g):
    batch, seq, d = x.shape
    assert d == D_MODEL and seq % FFN_TM == 0 and seq % FX_TQ == 0 and seq % ML_CHUNK == 0
    depth = ada_w.shape[0]
    assert depth == 2 and batch <= SUBLANES

    mod = _modulation(c, ada_w, ada_b)
    w_in_b = ffn_w_in.astype(BF16)
    w_out_b = ffn_w_out.astype(BF16)
    x2 = x.reshape(batch * seq, d)

    x2 = _ffn(x2, mod, 0, 0, batch, norm_g[0, 0], w_in_b[0, 0], w_out_b[0, 0])

    H, dk, dv = ML_HEADS, ML_DQK, ML_DV
    w = ml_w_in[0]
    n_qk = H * dk
    w_a = _pair_heads(w[:, :n_qk], w[:, n_qk:2 * n_qk], H, dk)
    w_v = w[:, 2 * n_qk:2 * n_qk + H * dv]
    w_o = w[:, 2 * n_qk + H * dv:2 * n_qk + 2 * H * dv]
    w_g = w[:, 2 * n_qk + 2 * H * dv:]
    w_vo_t = jnp.concatenate([w_v, w_o], axis=1).T.astype(BF16)
    cw = ml_conv_w[0]
    conv_w = _pair_heads(cw[:, :n_qk], cw[:, n_qk:], H, dk)
    bias = jnp.concatenate([ml_b_i[0], ml_b_f[0]])
    a_pre, vo_t, gates_t = _proj(
        x2, mod, 0, batch, norm_g[0, 1], w_a.astype(BF16), (2 * n_qk,), (BF16,),
        (w_vo_t, w_g.T.astype(BF16)), (BF16, F32))
    hs_t = _mlstm(a_pre, vo_t, gates_t, conv_w, bias, ml_hnorm_g[0], batch)
    x2 = _ffn(x2, mod, 0, 2, batch, norm_g[0, 2], w_in_b[0, 1], w_out_b[0, 1],
              mixer_out=hs_t, w_mix=ml_w_out[0].astype(BF16))

    x2 = _ffn(x2, mod, 1, 0, batch, norm_g[1, 0], w_in_b[1, 0], w_out_b[1, 0])

    H, dh = FX_HEADS, FX_DH
    w = fx_w_in[0]
    n = H * dh
    w_a = _pair_heads(w[:, :n], w[:, n:2 * n], H, dh)
    w_main = jnp.concatenate([w_a, _pad_lanes(w[:, 4 * n:])], axis=1)
    a, fgate, vog_t = _proj(
        x2, mod, 1, batch, norm_g[1, 1], w_main.astype(BF16), (2 * n, LANES),
        (BF16, F32), (w[:, 2 * n:4 * n].T.astype(BF16),), (BF16,))
    gqk = jnp.concatenate(
        [fx_qnorm_g[0] * (dh ** -0.5 * LOG2E), fx_knorm_g[0]]).reshape(1, LANES)
    bf_row = _pad_lanes(fx_b_f[0].reshape(1, H))
    q_aug, k_aug, f_ends = _fox_prep(a, fgate, bf_row, gqk, batch)
    qk_bound = (FX_BOUND_SLACK * dh * dh ** -0.5 * LOG2E
                * jnp.max(jnp.abs(fx_qnorm_g[0])) * jnp.max(jnp.abs(fx_knorm_g[0])))
    hperm = jnp.argsort(fx_b_f[0]).astype(jnp.int32)
    attn_t = _fox_attn(q_aug, k_aug, vog_t, f_ends, qk_bound, hperm, batch)
    w_mix = fx_w_out[0].reshape(H, dh, d)[hperm].reshape(H * dh, d).astype(BF16)
    x2 = _ffn(x2, mod, 1, 2, batch, norm_g[1, 2], w_in_b[1, 1], w_out_b[1, 1], final_g=final_g,
              mixer_out=attn_t, w_mix=w_mix)
    return x2.reshape(batch, seq, d)
```

```python
import functools

import jax
import jax.numpy as jnp
from jax import lax
from jax.experimental import pallas as pl
from jax.experimental.pallas import tpu as pltpu

D_MODEL = 1024
N_MOD = 9
EPS = 1e-6
FFN_RES = 0.5
CONV_W = 4
ML_HEADS = 8
ML_DV = 128
ML_DQK = 64
FX_HEADS = 16
FX_DH = 64
NEG = -1e30

LANES = 128
SUBLANES = 8
VMEM_LIMIT_BYTES = 56 * 1024 * 1024

MOD_TN = 1024
FFN_TM = 1024
FFN_TF = 1024
PROJ_TM = 512
PROJ_TN = 512
ML_CHUNK = 256
ML_HEAD_GROUP = 4
ML_ONES_ROWS = 16
FX_TS = 512
FX_TQ = 512
FX_TK = 512
FX_BIAS_LANE = FX_DH
FX_PRUNE_GAP = 170.0
FX_BOUND_SLACK = 1.05
FX_AHEAD = 1
FX_BUFS = FX_AHEAD + 1
FX_UNROLL = 8
FX_ONES_ROWS = 16
LOG2E = 1.4426950408889634

BF16 = jnp.bfloat16
F32 = jnp.float32


def _cparams(sem):
    return pltpu.CompilerParams(dimension_semantics=sem, vmem_limit_bytes=VMEM_LIMIT_BYTES)


def _dot(a, b):
    return jnp.dot(a, b, preferred_element_type=F32)


def _dot_nt(a, b):
    return lax.dot_general(a, b, (((1,), (1,)), ((), ())), preferred_element_type=F32)


def _dot_tn(a, b):
    return lax.dot_general(a, b, (((0,), (0,)), ((), ())), preferred_element_type=F32)


def _sigmoid(x):
    return 1.0 / (1.0 + jnp.exp(-x))


def _log_sigmoid(x):
    return jnp.minimum(x, 0.0) - jnp.log(1.0 + jnp.exp(-jnp.abs(x)))


def _split3(x):
    hi = x.astype(BF16)
    r1 = x - hi.astype(F32)
    mid = r1.astype(BF16)
    lo = (r1 - mid.astype(F32)).astype(BF16)
    return hi, mid, lo


def _adaln(x, g, shift, scale):
    ms = jnp.mean(x * x, axis=-1, keepdims=True)
    return (x * lax.rsqrt(ms + EPS)) * (g * (1.0 + scale)) + shift


def _mod_kernel(c_ref, w_ref, b_ref, o_ref):
    c = c_ref[...]
    cond = (c * _sigmoid(c)).astype(BF16)
    o_ref[0] = _dot(cond, w_ref[0].astype(BF16)) + b_ref[0]


def _modulation(c, ada_w, ada_b):
    depth, d, n = ada_w.shape
    b = c.shape[0]
    c_pad = jnp.zeros((SUBLANES, d), F32).at[:b].set(c)
    tn = MOD_TN
    out = pl.pallas_call(
        _mod_kernel,
        grid=(depth, n // tn),
        in_specs=[
            pl.BlockSpec((SUBLANES, d), lambda l, j: (0, 0)),
            pl.BlockSpec((1, d, tn), lambda l, j: (l, 0, j)),
            pl.BlockSpec((1, 1, tn), lambda l, j: (l, 0, j)),
        ],
        out_specs=pl.BlockSpec((1, SUBLANES, tn), lambda l, j: (l, 0, j)),
        out_shape=jax.ShapeDtypeStruct((depth, SUBLANES, n), F32),
        compiler_params=_cparams(("arbitrary", "arbitrary")),
        name="adaln_mod",
    )(c_pad, ada_w, ada_b.reshape(depth, 1, n))
    mod = out[:, :b].reshape(depth, b, N_MOD, d).transpose(0, 2, 1, 3)
    return mod.reshape(depth * N_MOD * b, 1, d)


def _mod_spec(layer, idx, batch, rows_per_batch_tiles):
    base = (layer * N_MOD + idx) * batch
    return pl.BlockSpec((None, 1, D_MODEL),
                        lambda m, *_: (base + m // rows_per_batch_tiles, 0, 0))


def _ffn_kernel(*refs, final_norm, mixer):
    refs = list(refs)
    x_ref, sh_ref, sc_ref, gt_ref, g_ref, wg_ref, wu_ref, wo_ref = refs[:8]
    del refs[:8]
    fg_ref = refs.pop(0) if final_norm else None
    if mixer is not None:
        a_ref, wmix_ref, gmix_ref = refs[:3]
        del refs[:3]
        o_ref, h_scr, acc_scr, x1_scr = refs
    else:
        o_ref, h_scr, acc_scr = refs
    f = pl.program_id(1)

    @pl.when(f == 0)
    def _():
        x1 = x_ref[...]
        if mixer == "rows":
            y = _dot(a_ref[...], wmix_ref[...])
        elif mixer == "time_minor":
            y = jnp.concatenate(
                [_dot_tn(a_ref[j], wmix_ref[...]) for j in range(a_ref.shape[0])], axis=0)
        if mixer is not None:
            x1 = x1 + (1.0 + gmix_ref[...]) * y
            x1_scr[...] = x1
        h = _adaln(x1, g_ref[...], sh_ref[...], sc_ref[...])
        h_scr[...] = h.astype(BF16)
        acc_scr[...] = jnp.zeros_like(acc_scr)

    h = h_scr[...]
    gate = _dot(h, wg_ref[...])
    up = _dot(h, wu_ref[...])
    a = (gate * _sigmoid(gate) * up).astype(BF16)
    acc_scr[...] += _dot(a, wo_ref[...])

    @pl.when(f == pl.num_programs(1) - 1)
    def _():
        x1 = x_ref[...] if mixer is None else x1_scr[...]
        out = x1 + (FFN_RES * (1.0 + gt_ref[...])) * acc_scr[...]
        if final_norm:
            ms = jnp.mean(out * out, axis=-1, keepdims=True)
            out = out * lax.rsqrt(ms + EPS) * fg_ref[...]
        o_ref[...] = out


def _ffn_ahead_kernel(x_ref, sh_ref, sc_ref, gt_ref, g_ref, wg_ref, wu_ref, wo_ref,
                      xn_ref, shn_ref, scn_ref, o_ref, h_scr, acc_scr):
    i = pl.program_id(0)
    f = pl.program_id(1)
    last = pl.num_programs(1) - 1

    @pl.when((f == 0) & (i == 0))
    def _():
        h_scr[...] = _adaln(x_ref[...], g_ref[...], sh_ref[...], sc_ref[...]).astype(BF16)

    def hidden():
        h = h_scr[...]
        gate = _dot(h, wg_ref[...])
        up = _dot(h, wu_ref[...])
        return (gate * _sigmoid(gate) * up).astype(BF16)

    @pl.when(f == 0)
    def _():
        acc_scr[...] = _dot(hidden(), wo_ref[...])

    @pl.when((f > 0) & (f < last))
    def _():
        acc_scr[...] += _dot(hidden(), wo_ref[...])

    @pl.when(f == last)
    def _():
        a = hidden()
        h_scr[...] = _adaln(xn_ref[...], g_ref[...], shn_ref[...], scn_ref[...]).astype(BF16)
        acc = acc_scr[...] + _dot(a, wo_ref[...])
        o_ref[...] = x_ref[...] + (FFN_RES * (1.0 + gt_ref[...])) * acc


def _ffn_ahead(x2, mod, layer, sub, batch, g, w_in, w_out):
    m, d = x2.shape
    tm, tf = FFN_TM, FFN_TF
    nf = w_out.shape[0] // tf
    nm = m // tm
    assert nf >= 2
    tiles_per_batch = (m // batch) // tm
    row = lambda m_, f_: (m_, 0)
    nxt = lambda m_, f_: (jnp.minimum(m_ + 1, nm - 1), 0)
    const = lambda m_, f_: (0, 0)

    def mod_next(idx):
        base = (layer * N_MOD + idx) * batch
        return pl.BlockSpec(
            (None, 1, d),
            lambda m_, f_: (base + jnp.minimum(m_ + 1, nm - 1) // tiles_per_batch, 0, 0))

    return pl.pallas_call(
        _ffn_ahead_kernel,
        grid=(nm, nf),
        in_specs=[
            pl.BlockSpec((tm, d), row),
            _mod_spec(layer, sub * 3 + 0, batch, tiles_per_batch),
            _mod_spec(layer, sub * 3 + 1, batch, tiles_per_batch),
            _mod_spec(layer, sub * 3 + 2, batch, tiles_per_batch),
            pl.BlockSpec((1, d), const),
            pl.BlockSpec((d, tf), lambda m_, f_: (0, f_)),
            pl.BlockSpec((d, tf), lambda m_, f_: (0, nf + f_)),
            pl.BlockSpec((tf, d), lambda m_, f_: (f_, 0)),
            pl.BlockSpec((tm, d), nxt),
            mod_next(sub * 3 + 0),
            mod_next(sub * 3 + 1),
        ],
        out_specs=pl.BlockSpec((tm, d), row),
        out_shape=jax.ShapeDtypeStruct((m, d), F32),
        scratch_shapes=[pltpu.VMEM((tm, d), BF16), pltpu.VMEM((tm, d), F32)],
        compiler_params=_cparams(("arbitrary", "arbitrary")),
        name="ffn_ahead",
    )(x2, mod, mod, mod, g.reshape(1, d), w_in, w_in, w_out, x2, mod, mod)


def _ffn(x2, mod, layer, sub, batch, g, w_in, w_out, final_g=None, mixer_out=None, w_mix=None):
    if final_g is None and mixer_out is None:
        return _ffn_ahead(x2, mod, layer, sub, batch, g, w_in, w_out)
    m, d = x2.shape
    f_dim = w_out.shape[0]
    tm, tf = FFN_TM, FFN_TF
    nf = f_dim // tf
    tiles_per_batch = (m // batch) // tm
    final_norm = final_g is not None
    row = lambda m_, f_: (m_, 0)
    const = lambda m_, f_: (0, 0)
    in_specs = [
        pl.BlockSpec((tm, d), row),
        _mod_spec(layer, sub * 3 + 0, batch, tiles_per_batch),
        _mod_spec(layer, sub * 3 + 1, batch, tiles_per_batch),
        _mod_spec(layer, sub * 3 + 2, batch, tiles_per_batch),
        pl.BlockSpec((1, d), const),
        pl.BlockSpec((d, tf), lambda m_, f_: (0, f_)),
        pl.BlockSpec((d, tf), lambda m_, f_: (0, nf + f_)),
        pl.BlockSpec((tf, d), lambda m_, f_: (f_, 0)),
    ]
    args = [x2, mod, mod, mod, g.reshape(1, d), w_in, w_in, w_out]
    scratch = [pltpu.VMEM((tm, d), BF16), pltpu.VMEM((tm, d), F32)]
    if final_norm:
        in_specs.append(pl.BlockSpec((1, d), const))
        args.append(final_g.reshape(1, d))
    mixer = None
    if mixer_out is not None:
        if mixer_out.ndim == 3:
            mixer = "time_minor"
            sub_tiles = tm // PROJ_TM
            in_specs.append(pl.BlockSpec((sub_tiles, mixer_out.shape[1], PROJ_TM),
                                         lambda m_, f_: (m_, 0, 0)))
        else:
            mixer = "rows"
            in_specs.append(pl.BlockSpec((tm, mixer_out.shape[1]), row))
        in_specs += [pl.BlockSpec(w_mix.shape, const),
                     _mod_spec(layer, 5, batch, tiles_per_batch)]
        args += [mixer_out, w_mix, mod]
        scratch.append(pltpu.VMEM((tm, d), F32))
    return pl.pallas_call(
        functools.partial(_ffn_kernel, final_norm=final_norm, mixer=mixer),
        grid=(m // tm, nf),
        in_specs=in_specs,
        out_specs=pl.BlockSpec((tm, d), row),
        out_shape=jax.ShapeDtypeStruct((m, d), F32),
        scratch_shapes=scratch,
        compiler_params=_cparams(("arbitrary", "arbitrary")),
        name="ffn_final" if final_norm else ("ffn" if mixer is None else "ffn_mix"),
    )(*args)


def _proj_kernel(x_ref, sh_ref, sc_ref, g_ref, w_ref, *refs, widths, n_t):
    h = _adaln(x_ref[...], g_ref[...], sh_ref[...], sc_ref[...]).astype(BF16)
    wt_refs = refs[:n_t]
    row_refs = refs[n_t:n_t + len(widths)]
    t_refs = refs[n_t + len(widths):]
    col = 0
    for o_ref, width in zip(row_refs, widths):
        step = min(width, PROJ_TN)
        for c0 in range(0, width, step):
            o_ref[:, c0:c0 + step] = _dot(
                h, w_ref[:, col + c0:col + c0 + step]).astype(o_ref.dtype)
        col += width
    for wt_ref, t_ref in zip(wt_refs, t_refs):
        rows = wt_ref.shape[0]
        step = min(rows, PROJ_TN)
        for r0 in range(0, rows, step):
            t_ref[0, r0:r0 + step, :] = _dot_nt(wt_ref[r0:r0 + step, :], h).astype(t_ref.dtype)


def _proj(x2, mod, layer, batch, g, w_main, widths, row_dtypes, w_ts, t_dtypes):
    m, d = x2.shape
    tm = PROJ_TM
    tiles_per_batch = (m // batch) // tm
    row = lambda m_: (m_, 0)
    const = lambda m_: (0, 0)
    out_shape = [jax.ShapeDtypeStruct((m, w), dt) for w, dt in zip(widths, row_dtypes)]
    out_specs = [pl.BlockSpec((tm, w), row) for w in widths]
    for w_t, dt in zip(w_ts, t_dtypes):
        out_shape.append(jax.ShapeDtypeStruct((m // tm, w_t.shape[0], tm), dt))
        out_specs.append(pl.BlockSpec((1, w_t.shape[0], tm), lambda m_: (m_, 0, 0)))
    return pl.pallas_call(
        functools.partial(_proj_kernel, widths=tuple(widths), n_t=len(w_ts)),
        grid=(m // tm,),
        in_specs=[
            pl.BlockSpec((tm, d), row),
            _mod_spec(layer, 3, batch, tiles_per_batch),
            _mod_spec(layer, 4, batch, tiles_per_batch),
            pl.BlockSpec((1, d), const),
            pl.BlockSpec(w_main.shape, const),
        ] + [pl.BlockSpec(w_t.shape, const) for w_t in w_ts],
        out_specs=out_specs,
        out_shape=out_shape,
        compiler_params=_cparams(("arbitrary",)),
        name="mixer_proj",
    )(x2, mod, mod, g.reshape(1, d), w_main, *w_ts)


def _mlstm_kernel(a_ref, vo_ref, gt_ref, cw_ref, bias_ref, hg_ref, out_ref, xbuf, c_scr, m_scr):
    L = ML_CHUNK
    H = ML_HEADS
    d = a_ref.shape[1]
    c = pl.program_id(1)

    @pl.when(c == 0)
    def _():
        xbuf[0:SUBLANES, :] = jnp.zeros((SUBLANES, xbuf.shape[1]), F32)
        c_scr[...] = jnp.zeros_like(c_scr)
        m_scr[...] = jnp.zeros_like(m_scr)

    xbuf[SUBLANES:SUBLANES + L, :] = a_ref[...].astype(F32)
    y = cw_ref[CONV_W - 1:CONV_W, :] * xbuf[SUBLANES:SUBLANES + L, :]
    for j in range(CONV_W - 1):
        off = SUBLANES - (CONV_W - 1) + j
        y = y + cw_ref[j:j + 1, :] * xbuf[off:off + L, :]
    xbuf[0:SUBLANES, :] = xbuf[L:L + SUBLANES, :]
    act = y * _sigmoid(y)

    gt = gt_ref[0] + bias_ref[...]
    lf_all = _log_sigmoid(gt) * LOG2E
    s_i = lax.broadcasted_iota(jnp.int32, (L, L), 0)
    t_i = lax.broadcasted_iota(jnp.int32, (L, L), 1)
    causal_t = s_i <= t_i
    tup = jnp.where(causal_t, 1.0, 0.0).astype(BF16)
    r_hi, r_mid, r_lo = _split3(lf_all)
    b = (_dot(r_hi, tup) + _dot(r_mid, tup) + _dot(r_lo, tup))[H:2 * H]
    i2 = gt[0:H] * LOG2E
    rowv = i2 - b
    lane = lax.broadcasted_iota(jnp.int32, (H, L), 1)
    cmax = rowv
    shift = 1
    while shift < L:
        cmax = jnp.maximum(cmax, jnp.where(lane >= shift, pltpu.roll(cmax, shift, 1), NEG))
        shift *= 2
    m_prev = m_scr[...]
    u = jnp.maximum(m_prev, cmax)
    w_inter = jnp.exp2(m_prev - u)
    inv_floor = jnp.exp2(-(b + u))
    g_tot = jnp.broadcast_to(b[:, L - 1:L], (H, L))
    a_w = g_tot - b + i2
    m_new = jnp.maximum(g_tot + m_prev, jnp.max(a_w, axis=1, keepdims=True))
    wk = jnp.exp2(a_w - m_new)
    decay = jnp.exp2(g_tot + m_prev - m_new)
    m_scr[...] = m_new

    rv_parts = [p.astype(F32) for p in _split3(rowv)]
    nu_parts = [p.astype(F32) for p in _split3(-u)]
    sub16 = lax.broadcasted_iota(jnp.int32, (2 * SUBLANES, L), 0)
    row16 = lambda x, h_: jnp.broadcast_to(x[h_:h_ + 1, :], (2 * SUBLANES, L))

    lane128 = lax.broadcasted_iota(jnp.int32, (L, LANES), 1)
    first_half = lane128 < ML_DQK
    ones_rows = jnp.ones((ML_ONES_ROWS, L), BF16)

    for h0 in range(0, H, ML_HEAD_GROUP):
        group = range(h0, h0 + ML_HEAD_GROUP)
        qbs, kbs, zs, s_ts, vas, nums = {}, {}, {}, {}, {}, {}
        for h in group:
            a_h = act[:, h * LANES:(h + 1) * LANES]
            qbs[h] = (jnp.where(first_half, a_h, 0.0) * (ML_DQK ** -0.5)).astype(BF16)
            kbs[h] = jnp.where(first_half, pltpu.roll(a_h, ML_DQK, 1), 0.0).astype(BF16)
        for h in group:
            src = jnp.where(sub16 == 0, row16(rv_parts[0], h), jnp.where(
                sub16 == 1, row16(rv_parts[1], h), jnp.where(
                    sub16 == 2, row16(rv_parts[2], h), jnp.where(sub16 < 6, 1.0, 0.0))))
            dst = jnp.where(sub16 < 3, 1.0, jnp.where(
                sub16 == 3, row16(nu_parts[0], h), jnp.where(
                    sub16 == 4, row16(nu_parts[1], h), jnp.where(
                        sub16 == 5, row16(nu_parts[2], h), 0.0))))
            zs[h] = _dot_tn(src.astype(BF16), dst.astype(BF16))
        for h in group:
            w_intra = jnp.exp2(jnp.where(causal_t, zs[h], NEG))
            s_ts[h] = (_dot_nt(kbs[h], qbs[h]) * w_intra).astype(BF16)
        for h in group:
            vas[h] = jnp.concatenate(
                [vo_ref[0, h * ML_DV:(h + 1) * ML_DV, :], ones_rows], axis=0)
            nums[h] = (_dot(vas[h], s_ts[h])
                       + _dot_nt(c_scr[h].astype(BF16), qbs[h]) * w_inter[h:h + 1, :])
        for h in group:
            num = nums[h]
            den = num[ML_DV:ML_DV + 1, :]
            h_t = num[:ML_DV, :] / jnp.maximum(jnp.abs(den), inv_floor[h:h + 1, :])
            hn = h_t * lax.rsqrt(jnp.mean(h_t * h_t, axis=0, keepdims=True) + EPS)
            hn = hn * hg_ref[h * ML_DV:(h + 1) * ML_DV, :]
            og = vo_ref[0, d + h * ML_DV:d + (h + 1) * ML_DV, :].astype(F32)
            out_ref[0, h * ML_DV:(h + 1) * ML_DV, :] = (_sigmoid(og) * hn).astype(out_ref.dtype)
        for h in group:
            vw = (vas[h].astype(F32) * wk[h:h + 1, :]).astype(BF16)
            c_scr[h] = decay[h:h + 1, 0:LANES] * c_scr[h] + _dot(vw, kbs[h])


def _mlstm(a_pre, vo_t, gates_t, conv_w, bias, hnorm_g, batch):
    m, d = a_pre.shape
    seq = m // batch
    L = ML_CHUNK
    nc = seq // L
    sub = PROJ_TM // L
    const = lambda b, c: (0, 0)
    tmin = lambda b, c: ((b * nc + c) // sub, 0, (b * nc + c) % sub)
    hg = jnp.broadcast_to(hnorm_g.reshape(d, 1), (d, L))
    bias_b = jnp.broadcast_to(bias.reshape(2 * ML_HEADS, 1), (2 * ML_HEADS, L))
    return pl.pallas_call(
        _mlstm_kernel,
        grid=(batch, nc),
        in_specs=[
            pl.BlockSpec((L, d), lambda b, c: (b * nc + c, 0)),
            pl.BlockSpec((1, 2 * d, L), tmin),
            pl.BlockSpec((1, 2 * ML_HEADS, L), tmin),
            pl.BlockSpec(conv_w.shape, const),
            pl.BlockSpec(bias_b.shape, const),
            pl.BlockSpec(hg.shape, const),
        ],
        out_specs=pl.BlockSpec((1, d, L), tmin),
        out_shape=jax.ShapeDtypeStruct((m // PROJ_TM, d, PROJ_TM), BF16),
        scratch_shapes=[
            pltpu.VMEM((SUBLANES + L, d), F32),
            pltpu.VMEM((ML_HEADS, ML_DV + ML_ONES_ROWS, LANES), F32),
            pltpu.VMEM((ML_HEADS, L), F32),
        ],
        compiler_params=_cparams(("arbitrary", "arbitrary")),
        name="mlstm_scan",
    )(a_pre, vo_t, gates_t, conv_w, bias_b, hg)


def _fox_prep_kernel(a_ref, f_ref, bf_ref, gqk_ref, place_ref, q_ref, k_ref, ends_ref, carry):
    ts = FX_TS
    H = FX_HEADS
    t = pl.program_id(1)

    @pl.when(t == 0)
    def _():
        carry[...] = jnp.zeros_like(carry)

    logf = _log_sigmoid(f_ref[...] + bf_ref[...]) * LOG2E
    r_i = lax.broadcasted_iota(jnp.int32, (ts, ts), 0)
    c_i = lax.broadcasted_iota(jnp.int32, (ts, ts), 1)
    tlow = jnp.where(r_i >= c_i, 1.0, 0.0).astype(BF16)
    hi, mid, lo = _split3(logf)
    fcum = _dot(tlow, hi) + _dot(tlow, mid) + _dot(tlow, lo) + carry[0:1, :]
    carry[...] = jnp.broadcast_to(fcum[ts - 1:ts, :], carry.shape)

    sub = lax.broadcasted_iota(jnp.int32, (SUBLANES, LANES), 0)
    ends_ref[0] = jnp.where(
        sub == 0, jnp.broadcast_to(fcum[0:1, :], (SUBLANES, LANES)),
        jnp.broadcast_to(fcum[ts - 1:ts, :], (SUBLANES, LANES)))

    lane = lax.broadcasted_iota(jnp.int32, (ts, LANES), 1)
    f_m = jnp.where(lane < H, fcum, 0.0)
    f_hi = f_m.astype(BF16).astype(F32)
    f_r1 = f_m - f_hi
    f_mid = f_r1.astype(BF16).astype(F32)
    f_lo = f_r1 - f_mid
    f_parts = (f_hi + pltpu.roll(f_mid, H, 1) + pltpu.roll(f_lo, 2 * H, 1)
               + jnp.where(lane == 3 * H, 1.0, 0.0)).astype(BF16)

    first_half = lane < FX_DH
    gqk = gqk_ref[...]
    for h in range(H):
        a = a_ref[:, h * LANES:(h + 1) * LANES].astype(F32)
        lhs = jnp.concatenate([(a * a).astype(BF16), f_parts], axis=1)
        res = _dot(lhs, place_ref[h])
        rinv = lax.rsqrt(res[:, :LANES] * (1.0 / FX_DH) + EPS)
        bias = res[:, LANES:]
        a_n = a * rinv * gqk
        q_ref[:, h * LANES:(h + 1) * LANES] = jnp.where(first_half, a_n, bias).astype(BF16)
        k_ref[:, h * LANES:(h + 1) * LANES] = pltpu.roll(
            jnp.where(first_half, bias, a_n), FX_DH, 1).astype(BF16)


def _fox_placement():
    H = FX_HEADS
    r = lax.broadcasted_iota(jnp.int32, (H, 2 * LANES, 2 * LANES), 1)
    c = lax.broadcasted_iota(jnp.int32, (H, 2 * LANES, 2 * LANES), 2)
    h = lax.broadcasted_iota(jnp.int32, (H, 2 * LANES, 2 * LANES), 0)
    half = (r < LANES) & (c < LANES) & ((r < FX_DH) == (c < FX_DH))
    src = r - LANES
    dst = c - LANES
    qb = FX_BIAS_LANE
    one_src = src == 3 * H
    plus = ((src == h) & (dst == qb)) | ((src == H + h) & (dst == qb + 1)) | (
        (src == 2 * H + h) & (dst == qb + 2)) | (one_src & (dst >= qb + 3) & (dst < qb + 6)) | (
        one_src & (dst >= 0) & (dst < 3))
    minus = ((src == h) & (dst == 3)) | ((src == H + h) & (dst == 4)) | (
        (src == 2 * H + h) & (dst == 5))
    in_place = (r >= LANES) & (c >= LANES)
    val = jnp.where(half | (in_place & plus), 1.0, jnp.where(in_place & minus, -1.0, 0.0))
    return val.astype(BF16)


def _fox_prep(a, fgate, bf_row, gqk, batch):
    m = a.shape[0]
    seq = m // batch
    ts = FX_TS
    nt = seq // ts
    wide = FX_HEADS * LANES
    row = lambda b, t: (b * nt + t, 0)
    const = lambda b, t: (0, 0)
    return pl.pallas_call(
        _fox_prep_kernel,
        grid=(batch, nt),
        in_specs=[
            pl.BlockSpec((ts, wide), row),
            pl.BlockSpec((ts, LANES), row),
            pl.BlockSpec((1, LANES), const),
            pl.BlockSpec((1, LANES), const),
            pl.BlockSpec((FX_HEADS, 2 * LANES, 2 * LANES), lambda b, t: (0, 0, 0)),
        ],
        out_specs=[pl.BlockSpec((ts, wide), row)] * 2
        + [pl.BlockSpec((1, SUBLANES, LANES), lambda b, t: (b * nt + t, 0, 0))],
        out_shape=[jax.ShapeDtypeStruct((m, wide), BF16)] * 2
        + [jax.ShapeDtypeStruct((m // ts, SUBLANES, LANES), F32)],
        scratch_shapes=[pltpu.VMEM((SUBLANES, LANES), F32)],
        compiler_params=_cparams(("arbitrary", "arbitrary")),
        name="fox_prep",
    )(a, fgate, bf_row, gqk, _fox_placement())


def _fox_attn_kernel(qtab_ref, ktab_ref, nit_ref, hperm_ref, qa_ref, qb_ref, ka_ref, kb_ref,
                     va_ref, vb_ref, oga_ref, ogb_ref, o_ref, m_st, acc_st, *score_scr,
                     n_diag, stride):
    tq, tk = FX_TQ, FX_TK
    del hperm_ref
    q_refs, k_refs = (qa_ref, qb_ref), (ka_ref, kb_ref)
    v_refs, og_refs = (va_ref, vb_ref), (oga_ref, ogb_ref)
    step = pl.program_id(0) * pl.num_programs(1) + pl.program_id(1)
    base = step * stride
    bufs = [(score_scr[2 * j], score_scr[2 * j + 1]) for j in range(FX_BUFS)]
    key_i = lax.broadcasted_iota(jnp.int32, (tk, tq), 0)
    qry_i = lax.broadcasted_iota(jnp.int32, (tk, tq), 1)
    causal = key_i <= qry_i
    ones_rows = jnp.ones((FX_ONES_ROWS, tk), BF16)

    def score(e, s_ref, mx_ref):
        q0 = pl.multiple_of(qtab_ref[base + e] * tq, tq)
        k0 = pl.multiple_of(ktab_ref[base + e] * tk, tk)
        for hh in range(2):
            q = q_refs[hh][0, pl.ds(q0, tq), :]
            k = k_refs[hh][0, pl.ds(k0, tk), :]
            s = _dot_nt(k, q)
            s_ref[hh] = s
            mx_ref[hh] = jnp.max(s, axis=0, keepdims=True)

    def values(kj, hh):
        return jnp.concatenate([v_refs[hh][0, kj], ones_rows], axis=0)

    def consume_diag(e, s_ref):
        qi = qtab_ref[base + e]
        for hh in range(2):
            s = jnp.where(causal, s_ref[hh], NEG)
            m_new = jnp.max(s, axis=0, keepdims=True)
            p = jnp.exp2(s - m_new).astype(BF16)
            acc_st[qi, hh] = _dot(values(qi, hh), p)
            m_st[qi, hh] = m_new

    def consume(e, s_ref, mx_ref):
        qi = qtab_ref[base + e]
        kj = ktab_ref[base + e]
        for hh in range(2):
            m_prev = m_st[qi, hh]
            m_new = jnp.maximum(m_prev, mx_ref[hh])
            p = jnp.exp2(s_ref[hh] - m_new).astype(BF16)
            corr = jnp.exp2(m_prev - m_new)
            acc_st[qi, hh] = corr * acc_st[qi, hh] + _dot(values(kj, hh), p)
            m_st[qi, hh] = m_new

    def diag_body(i, carry):
        for j in range(FX_UNROLL):
            e = FX_UNROLL * i + j
            score(e + FX_AHEAD, *bufs[(j + FX_AHEAD) % FX_BUFS])
            consume_diag(e, bufs[j % FX_BUFS][0])
        return carry

    def body(i, carry):
        for j in range(FX_UNROLL):
            e = FX_UNROLL * i + j
            score(e + FX_AHEAD, *bufs[(j + FX_AHEAD) % FX_BUFS])
            consume(e, *bufs[j % FX_BUFS])
        return carry

    for e in range(FX_AHEAD):
        score(e, *bufs[e])
    lax.fori_loop(0, n_diag // FX_UNROLL, diag_body, 0)
    lax.fori_loop(n_diag // FX_UNROLL, n_diag // FX_UNROLL + nit_ref[step], body, 0)

    def finalize(qi, carry):
        for hh in range(2):
            acc = acc_st[qi, hh]
            out = acc[:FX_DH, :] / acc[FX_DH:FX_DH + 1, :]
            og = og_refs[hh][0, qi].astype(F32)
            o_ref[0, qi, hh * FX_DH:(hh + 1) * FX_DH, :] = (_sigmoid(og) * out).astype(o_ref.dtype)
        return carry

    lax.fori_loop(0, n_diag, finalize, 0)


def _fox_block_tables(f_ends, qk_bound, hperm, batch, nq):
    pairs = FX_HEADS // 2
    f_first = f_ends[:, 0, :FX_HEADS].reshape(batch, nq, FX_HEADS)[:, :, hperm]
    f_last = f_ends[:, 1, :FX_HEADS].reshape(batch, nq, FX_HEADS)[:, :, hperm]
    f_first = f_first.reshape(batch, nq, pairs, 2)
    f_last = f_last.reshape(batch, nq, pairs, 2)
    qi = [i for i in range(nq) for _ in range(i)]
    kj = [j for i in range(nq) for j in range(i)]
    gap = f_last[:, jnp.asarray(kj)] - f_first[:, jnp.asarray(qi)]
    dead = jnp.all(gap > 2.0 * qk_bound + FX_PRUNE_GAP, axis=-1)
    dead = dead.transpose(0, 2, 1).astype(jnp.int32)
    full = lambda v: jnp.broadcast_to(jnp.asarray(v, jnp.int32), dead.shape)
    _, q_live_first, k_live_first = lax.sort(
        (dead, full(qi), full(kj)), dimension=-1, is_stable=True, num_keys=1)
    n_live = len(qi) - jnp.sum(dead, axis=-1)
    n_iter = (n_live + FX_UNROLL - 1) // FX_UNROLL
    diag = jnp.broadcast_to(jnp.arange(nq, dtype=jnp.int32), (batch, pairs, nq))
    pad = jnp.zeros((batch, pairs, FX_AHEAD), jnp.int32)
    qtab = jnp.concatenate([diag, q_live_first, pad], axis=-1)
    ktab = jnp.concatenate([diag, k_live_first, pad], axis=-1)
    return qtab.reshape(-1), ktab.reshape(-1), n_iter.reshape(-1), qtab.shape[-1]


def _fox_attn(q_aug, k_aug, vog_t, f_ends, qk_bound, hperm, batch):
    m, wide = q_aug.shape
    seq = m // batch
    tq, tk = FX_TQ, FX_TK
    assert tq == tk == PROJ_TM == FX_TS
    nq = seq // tq
    assert nq % FX_UNROLL == 0 and (nq * (nq - 1) // 2) % FX_UNROLL == 0
    pairs = FX_HEADS // 2
    H = FX_HEADS
    q3 = q_aug.reshape(batch, seq, wide)
    k3 = k_aug.reshape(batch, seq, wide)
    vog4 = vog_t.reshape(batch, nq, 2 * H * FX_DH, tk)
    qtab, ktab, n_iter, stride = _fox_block_tables(f_ends, qk_bound, hperm, batch, nq)
    rows = FX_DH + FX_ONES_ROWS

    def head_spec(shape, which, offset):
        if len(shape) == 3:
            return pl.BlockSpec(shape, lambda b, p, qt, kt, ni, hp: (b, 0, offset + hp[2 * p + which]))
        return pl.BlockSpec(shape, lambda b, p, qt, kt, ni, hp: (b, 0, offset + hp[2 * p + which], 0))

    grid_spec = pltpu.PrefetchScalarGridSpec(
        num_scalar_prefetch=4,
        grid=(batch, pairs),
        in_specs=[
            head_spec((1, seq, LANES), 0, 0), head_spec((1, seq, LANES), 1, 0),
            head_spec((1, seq, LANES), 0, 0), head_spec((1, seq, LANES), 1, 0),
            head_spec((1, nq, FX_DH, tk), 0, 0), head_spec((1, nq, FX_DH, tk), 1, 0),
            head_spec((1, nq, FX_DH, tk), 0, H), head_spec((1, nq, FX_DH, tk), 1, H),
        ],
        out_specs=pl.BlockSpec((1, nq, 2 * FX_DH, tq), lambda b, p, *_: (b, 0, p, 0)),
        scratch_shapes=[pltpu.VMEM((nq, 2, 1, tq), F32),
                        pltpu.VMEM((nq, 2, rows, tq), F32)]
        + [pltpu.VMEM((2, tk, tq), F32), pltpu.VMEM((2, 1, tq), F32)] * FX_BUFS,
    )
    out = pl.pallas_call(
        functools.partial(_fox_attn_kernel, n_diag=nq, stride=stride),
        grid_spec=grid_spec,
        out_shape=jax.ShapeDtypeStruct((batch, nq, H * FX_DH, tq), BF16),
        compiler_params=_cparams(("arbitrary", "arbitrary")),
        name="fox_attn",
    )(qtab, ktab, n_iter, hperm, q3, q3, k3, k3, vog4, vog4, vog4, vog4)
    return out.reshape(batch * nq, H * FX_DH, tq)


def _pair_heads(wq, wk, heads, dh):
    lead = wq.shape[0]
    return jnp.concatenate(
        [wq.reshape(lead, heads, dh), wk.reshape(lead, heads, dh)], axis=2
    ).reshape(lead, heads * 2 * dh)


def _pad_lanes(w):
    return jnp.zeros((w.shape[0], LANES), w.dtype).at[:, :w.shape[1]].set(w)


def kernel(x, c, ada_w, ada_b, norm_g, ffn_w_in, ffn_w_out, ml_w_in, ml_conv_w, ml_b_i, ml_b_f,
           ml_hnorm_g, ml_w_out, fx_w_in, fx_b_f, fx_qnorm_g, fx_knorm_g, fx_w_out, final_g):
    batch, seq, d = x.shape
    assert d == D_MODEL and seq % FFN_TM == 0 and seq % FX_TQ == 0 and seq % ML_CHUNK == 0
    depth = ada_w.shape[0]
    assert depth == 2 and batch <= SUBLANES

    mod = _modulation(c, ada_w, ada_b)
    w_in_b = ffn_w_in.astype(BF16)
    w_out_b = ffn_w_out.astype(BF16)
    x2 = x.reshape(batch * seq, d)

    x2 = _ffn(x2, mod, 0, 0, batch, norm_g[0, 0], w_in_b[0, 0], w_out_b[0, 0])

    H, dk, dv = ML_HEADS, ML_DQK, ML_DV
    w = ml_w_in[0]
    n_qk = H * dk
    w_a = _pair_heads(w[:, :n_qk], w[:, n_qk:2 * n_qk], H, dk)
    w_v = w[:, 2 * n_qk:2 * n_qk + H * dv]
    w_o = w[:, 2 * n_qk + H * dv:2 * n_qk + 2 * H * dv]
    w_g = w[:, 2 * n_qk + 2 * H * dv:]
    w_vo_t = jnp.concatenate([w_v, w_o], axis=1).T.astype(BF16)
    cw = ml_conv_w[0]
    conv_w = _pair_heads(cw[:, :n_qk], cw[:, n_qk:], H, dk)
    bias = jnp.concatenate([ml_b_i[0], ml_b_f[0]])
    a_pre, vo_t, gates_t = _proj(
        x2, mod, 0, batch, norm_g[0, 1], w_a.astype(BF16), (2 * n_qk,), (BF16,),
        (w_vo_t, w_g.T.astype(BF16)), (BF16, F32))
    hs_t = _mlstm(a_pre, vo_t, gates_t, conv_w, bias, ml_hnorm_g[0], batch)
    x2 = _ffn(x2, mod, 0, 2, batch, norm_g[0, 2], w_in_b[0, 1], w_out_b[0, 1],
              mixer_out=hs_t, w_mix=ml_w_out[0].astype(BF16))

    x2 = _ffn(x2, mod, 1, 0, batch, norm_g[1, 0], w_in_b[1, 0], w_out_b[1, 0])

    H, dh = FX_HEADS, FX_DH
    w = fx_w_in[0]
    n = H * dh
    w_a = _pair_heads(w[:, :n], w[:, n:2 * n], H, dh)
    w_main = jnp.concatenate([w_a, _pad_lanes(w[:, 4 * n:])], axis=1)
    a, fgate, vog_t = _proj(
        x2, mod, 1, batch, norm_g[1, 1], w_main.astype(BF16), (2 * n, LANES),
        (BF16, F32), (w[:, 2 * n:4 * n].T.astype(BF16),), (BF16,))
    gqk = jnp.concatenate(
        [fx_qnorm_g[0] * (dh ** -0.5 * LOG2E), fx_knorm_g[0]]).reshape(1, LANES)
    bf_row = _pad_lanes(fx_b_f[0].reshape(1, H))
    q_aug, k_aug, f_ends = _fox_prep(a, fgate, bf_row, gqk, batch)
    qk_bound = (FX_BOUND_SLACK * dh * dh ** -0.5 * LOG2E
                * jnp.max(jnp.abs(fx_qnorm_g[0])) * jnp.max(jnp.abs(fx_knorm_g[0])))
    hperm = jnp.argsort(fx_b_f[0]).astype(jnp.int32)
    attn_t = _fox_attn(q_aug, k_aug, vog_t, f_ends, qk_bound, hperm, batch)
    w_mix = fx_w_out[0].reshape(H, dh, d)[hperm].reshape(H * dh, d).astype(BF16)
    x2 = _ffn(x2, mod, 1, 2, batch, norm_g[1, 2], w_in_b[1, 1], w_out_b[1, 1], final_g=final_g,
              mixer_out=attn_t, w_mix=w_mix)
    return x2.reshape(batch, seq, d)
```

```python
import functools

import jax
import jax.numpy as jnp
from jax import lax
from jax.experimental import pallas as pl
from jax.experimental.pallas import tpu as pltpu

D_MODEL = 1024
N_MOD = 9
EPS = 1e-6
FFN_RES = 0.5
CONV_W = 4
ML_HEADS = 8
ML_DV = 128
ML_DQK = 64
FX_HEADS = 16
FX_DH = 64
NEG = -1e30

LANES = 128
SUBLANES = 8
VMEM_LIMIT_BYTES = 56 * 1024 * 1024

MOD_TN = 1024
FFN_TM = 1024
FFN_TF = 1024
PROJ_TM = 512
PROJ_TN = 512
ML_CHUNK = 256
ML_HEAD_GROUP = 4
ML_ONES_ROWS = 16
FX_TS = 512
FX_TQ = 512
FX_TK = 512
FX_BIAS_LANE = FX_DH
FX_PRUNE_GAP = 170.0
FX_BOUND_SLACK = 1.05
FX_AHEAD = 1
FX_BUFS = FX_AHEAD + 1
FX_UNROLL = 8
FX_ONES_ROWS = 16
LOG2E = 1.4426950408889634

BF16 = jnp.bfloat16
F32 = jnp.float32


def _cparams(sem):
    return pltpu.CompilerParams(dimension_semantics=sem, vmem_limit_bytes=VMEM_LIMIT_BYTES)


def _resident(shape, index_map):
    return pl.BlockSpec(shape, index_map, pipeline_mode=pl.Buffered(1))


def _dot(a, b):
    return jnp.dot(a, b, preferred_element_type=F32)


def _dot_nt(a, b):
    return lax.dot_general(a, b, (((1,), (1,)), ((), ())), preferred_element_type=F32)


def _dot_tn(a, b):
    return lax.dot_general(a, b, (((0,), (0,)), ((), ())), preferred_element_type=F32)


def _sigmoid(x):
    return 1.0 / (1.0 + jnp.exp(-x))


def _log_sigmoid(x):
    return jnp.minimum(x, 0.0) - jnp.log(1.0 + jnp.exp(-jnp.abs(x)))


def _split3(x):
    hi = x.astype(BF16)
    r1 = x - hi.astype(F32)
    mid = r1.astype(BF16)
    lo = (r1 - mid.astype(F32)).astype(BF16)
    return hi, mid, lo


def _adaln(x, g, shift, scale):
    ms = jnp.mean(x * x, axis=-1, keepdims=True)
    return (x * lax.rsqrt(ms + EPS)) * (g * (1.0 + scale)) + shift


def _mod_kernel(c_ref, w_ref, b_ref, o_ref):
    c = c_ref[...]
    cond = (c * _sigmoid(c)).astype(BF16)
    o_ref[0] = _dot(cond, w_ref[0].astype(BF16)) + b_ref[0]


def _modulation(c, ada_w, ada_b):
    depth, d, n = ada_w.shape
    b = c.shape[0]
    c_pad = jnp.zeros((SUBLANES, d), F32).at[:b].set(c)
    tn = MOD_TN
    out = pl.pallas_call(
        _mod_kernel,
        grid=(depth, n // tn),
        in_specs=[
            pl.BlockSpec((SUBLANES, d), lambda l, j: (0, 0)),
            pl.BlockSpec((1, d, tn), lambda l, j: (l, 0, j)),
            pl.BlockSpec((1, 1, tn), lambda l, j: (l, 0, j)),
        ],
        out_specs=pl.BlockSpec((1, SUBLANES, tn), lambda l, j: (l, 0, j)),
        out_shape=jax.ShapeDtypeStruct((depth, SUBLANES, n), F32),
        compiler_params=_cparams(("arbitrary", "arbitrary")),
        name="adaln_mod",
    )(c_pad, ada_w, ada_b.reshape(depth, 1, n))
    mod = out[:, :b].reshape(depth, b, N_MOD, d).transpose(0, 2, 1, 3)
    return mod.reshape(depth * N_MOD * b, 1, d)


def _mod_spec(layer, idx, batch, rows_per_batch_tiles):
    base = (layer * N_MOD + idx) * batch
    return pl.BlockSpec((None, 1, D_MODEL),
                        lambda m, *_: (base + m // rows_per_batch_tiles, 0, 0))


def _ffn_kernel(*refs, final_norm, mixer):
    refs = list(refs)
    x_ref, sh_ref, sc_ref, gt_ref, g_ref, wg_ref, wu_ref, wo_ref = refs[:8]
    del refs[:8]
    fg_ref = refs.pop(0) if final_norm else None
    if mixer is not None:
        a_ref, wmix_ref, gmix_ref = refs[:3]
        del refs[:3]
        o_ref, h_scr, acc_scr, x1_scr = refs
    else:
        o_ref, h_scr, acc_scr = refs
    f = pl.program_id(1)

    @pl.when(f == 0)
    def _():
        x1 = x_ref[...]
        if mixer == "rows":
            y = _dot(a_ref[...], wmix_ref[...])
        elif mixer == "time_minor":
            y = jnp.concatenate(
                [_dot_tn(a_ref[j], wmix_ref[...]) for j in range(a_ref.shape[0])], axis=0)
        if mixer is not None:
            x1 = x1 + (1.0 + gmix_ref[...]) * y
            x1_scr[...] = x1
        h = _adaln(x1, g_ref[...], sh_ref[...], sc_ref[...])
        h_scr[...] = h.astype(BF16)
        acc_scr[...] = jnp.zeros_like(acc_scr)

    h = h_scr[...]
    gate = _dot(h, wg_ref[...])
    up = _dot(h, wu_ref[...])
    a = (gate * _sigmoid(gate) * up).astype(BF16)
    acc_scr[...] += _dot(a, wo_ref[...])

    @pl.when(f == pl.num_programs(1) - 1)
    def _():
        x1 = x_ref[...] if mixer is None else x1_scr[...]
        out = x1 + (FFN_RES * (1.0 + gt_ref[...])) * acc_scr[...]
        if final_norm:
            ms = jnp.mean(out * out, axis=-1, keepdims=True)
            out = out * lax.rsqrt(ms + EPS) * fg_ref[...]
        o_ref[...] = out


def _ffn_ahead_kernel(x_ref, sh_ref, sc_ref, gt_ref, g_ref, wg_ref, wu_ref, wo_ref,
                      xn_ref, shn_ref, scn_ref, o_ref, h_scr, acc_scr):
    i = pl.program_id(0)
    f = pl.program_id(1)
    last = pl.num_programs(1) - 1

    @pl.when((f == 0) & (i == 0))
    def _():
        h_scr[...] = _adaln(x_ref[...], g_ref[...], sh_ref[...], sc_ref[...]).astype(BF16)

    def hidden():
        h = h_scr[...]
        gate = _dot(h, wg_ref[...])
        up = _dot(h, wu_ref[...])
        return (gate * _sigmoid(gate) * up).astype(BF16)

    @pl.when(f == 0)
    def _():
        acc_scr[...] = _dot(hidden(), wo_ref[...])

    @pl.when((f > 0) & (f < last))
    def _():
        acc_scr[...] += _dot(hidden(), wo_ref[...])

    @pl.when(f == last)
    def _():
        a = hidden()
        h_scr[...] = _adaln(xn_ref[...], g_ref[...], shn_ref[...], scn_ref[...]).astype(BF16)
        acc = acc_scr[...] + _dot(a, wo_ref[...])
        o_ref[...] = x_ref[...] + (FFN_RES * (1.0 + gt_ref[...])) * acc


def _ffn_ahead(x2, mod, layer, sub, batch, g, w_in, w_out):
    m, d = x2.shape
    tm, tf = FFN_TM, FFN_TF
    nf = w_out.shape[0] // tf
    nm = m // tm
    assert nf >= 2
    tiles_per_batch = (m // batch) // tm
    row = lambda m_, f_: (m_, 0)
    nxt = lambda m_, f_: (jnp.minimum(m_ + 1, nm - 1), 0)
    const = lambda m_, f_: (0, 0)

    def mod_next(idx):
        base = (layer * N_MOD + idx) * batch
        return pl.BlockSpec(
            (None, 1, d),
            lambda m_, f_: (base + jnp.minimum(m_ + 1, nm - 1) // tiles_per_batch, 0, 0))

    return pl.pallas_call(
        _ffn_ahead_kernel,
        grid=(nm, nf),
        in_specs=[
            pl.BlockSpec((tm, d), row),
            _mod_spec(layer, sub * 3 + 0, batch, tiles_per_batch),
            _mod_spec(layer, sub * 3 + 1, batch, tiles_per_batch),
            _mod_spec(layer, sub * 3 + 2, batch, tiles_per_batch),
            pl.BlockSpec((1, d), const),
            pl.BlockSpec((d, tf), lambda m_, f_: (0, f_)),
            pl.BlockSpec((d, tf), lambda m_, f_: (0, nf + f_)),
            pl.BlockSpec((tf, d), lambda m_, f_: (f_, 0)),
            pl.BlockSpec((tm, d), nxt),
            mod_next(sub * 3 + 0),
            mod_next(sub * 3 + 1),
        ],
        out_specs=pl.BlockSpec((tm, d), row),
        out_shape=jax.ShapeDtypeStruct((m, d), F32),
        scratch_shapes=[pltpu.VMEM((tm, d), BF16), pltpu.VMEM((tm, d), F32)],
        compiler_params=_cparams(("arbitrary", "arbitrary")),
        name="ffn_ahead",
    )(x2, mod, mod, mod, g.reshape(1, d), w_in, w_in, w_out, x2, mod, mod)


def _ffn(x2, mod, layer, sub, batch, g, w_in, w_out, final_g=None, mixer_out=None, w_mix=None):
    if final_g is None and mixer_out is None:
        return _ffn_ahead(x2, mod, layer, sub, batch, g, w_in, w_out)
    m, d = x2.shape
    f_dim = w_out.shape[0]
    tm, tf = FFN_TM, FFN_TF
    nf = f_dim // tf
    tiles_per_batch = (m // batch) // tm
    final_norm = final_g is not None
    row = lambda m_, f_: (m_, 0)
    const = lambda m_, f_: (0, 0)
    in_specs = [
        pl.BlockSpec((tm, d), row),
        _mod_spec(layer, sub * 3 + 0, batch, tiles_per_batch),
        _mod_spec(layer, sub * 3 + 1, batch, tiles_per_batch),
        _mod_spec(layer, sub * 3 + 2, batch, tiles_per_batch),
        pl.BlockSpec((1, d), const),
        pl.BlockSpec((d, tf), lambda m_, f_: (0, f_)),
        pl.BlockSpec((d, tf), lambda m_, f_: (0, nf + f_)),
        pl.BlockSpec((tf, d), lambda m_, f_: (f_, 0)),
    ]
    args = [x2, mod, mod, mod, g.reshape(1, d), w_in, w_in, w_out]
    scratch = [pltpu.VMEM((tm, d), BF16), pltpu.VMEM((tm, d), F32)]
    if final_norm:
        in_specs.append(pl.BlockSpec((1, d), const))
        args.append(final_g.reshape(1, d))
    mixer = None
    if mixer_out is not None:
        if mixer_out.ndim == 3:
            mixer = "time_minor"
            sub_tiles = tm // PROJ_TM
            in_specs.append(pl.BlockSpec((sub_tiles, mixer_out.shape[1], PROJ_TM),
                                         lambda m_, f_: (m_, 0, 0)))
        else:
            mixer = "rows"
            in_specs.append(pl.BlockSpec((tm, mixer_out.shape[1]), row))
        in_specs += [_resident(w_mix.shape, const),
                     _mod_spec(layer, 5, batch, tiles_per_batch)]
        args += [mixer_out, w_mix, mod]
        scratch.append(pltpu.VMEM((tm, d), F32))
    return pl.pallas_call(
        functools.partial(_ffn_kernel, final_norm=final_norm, mixer=mixer),
        grid=(m // tm, nf),
        in_specs=in_specs,
        out_specs=pl.BlockSpec((tm, d), row),
        out_shape=jax.ShapeDtypeStruct((m, d), F32),
        scratch_shapes=scratch,
        compiler_params=_cparams(("arbitrary", "arbitrary")),
        name="ffn_final" if final_norm else ("ffn" if mixer is None else "ffn_mix"),
    )(*args)


def _proj_kernel(x_ref, sh_ref, sc_ref, g_ref, w_ref, *refs, widths, n_t):
    h = _adaln(x_ref[...], g_ref[...], sh_ref[...], sc_ref[...]).astype(BF16)
    wt_refs = refs[:n_t]
    row_refs = refs[n_t:n_t + len(widths)]
    t_refs = refs[n_t + len(widths):]
    col = 0
    for o_ref, width in zip(row_refs, widths):
        step = min(width, PROJ_TN)
        for c0 in range(0, width, step):
            o_ref[:, c0:c0 + step] = _dot(
                h, w_ref[:, col + c0:col + c0 + step]).astype(o_ref.dtype)
        col += width
    for wt_ref, t_ref in zip(wt_refs, t_refs):
        rows = wt_ref.shape[0]
        step = min(rows, PROJ_TN)
        for r0 in range(0, rows, step):
            t_ref[0, r0:r0 + step, :] = _dot_nt(wt_ref[r0:r0 + step, :], h).astype(t_ref.dtype)


def _proj(x2, mod, layer, batch, g, w_main, widths, row_dtypes, w_ts, t_dtypes):
    m, d = x2.shape
    tm = PROJ_TM
    tiles_per_batch = (m // batch) // tm
    row = lambda m_: (m_, 0)
    const = lambda m_: (0, 0)
    out_shape = [jax.ShapeDtypeStruct((m, w), dt) for w, dt in zip(widths, row_dtypes)]
    out_specs = [pl.BlockSpec((tm, w), row) for w in widths]
    for w_t, dt in zip(w_ts, t_dtypes):
        out_shape.append(jax.ShapeDtypeStruct((m // tm, w_t.shape[0], tm), dt))
        out_specs.append(pl.BlockSpec((1, w_t.shape[0], tm), lambda m_: (m_, 0, 0)))
    return pl.pallas_call(
        functools.partial(_proj_kernel, widths=tuple(widths), n_t=len(w_ts)),
        grid=(m // tm,),
        in_specs=[
            pl.BlockSpec((tm, d), row),
            _mod_spec(layer, 3, batch, tiles_per_batch),
            _mod_spec(layer, 4, batch, tiles_per_batch),
            pl.BlockSpec((1, d), const),
            _resident(w_main.shape, const),
        ] + [_resident(w_t.shape, const) for w_t in w_ts],
        out_specs=out_specs,
        out_shape=out_shape,
        compiler_params=_cparams(("arbitrary",)),
        name="mixer_proj",
    )(x2, mod, mod, g.reshape(1, d), w_main, *w_ts)


def _mlstm_kernel(a_ref, vo_ref, gt_ref, cw_ref, bias_ref, hg_ref, out_ref, xbuf, c_scr, m_scr):
    L = ML_CHUNK
    H = ML_HEADS
    d = a_ref.shape[1]
    c = pl.program_id(1)

    @pl.when(c == 0)
    def _():
        xbuf[0:SUBLANES, :] = jnp.zeros((SUBLANES, xbuf.shape[1]), F32)
        c_scr[...] = jnp.zeros_like(c_scr)
        m_scr[...] = jnp.zeros_like(m_scr)

    xbuf[SUBLANES:SUBLANES + L, :] = a_ref[...].astype(F32)
    y = cw_ref[CONV_W - 1:CONV_W, :] * xbuf[SUBLANES:SUBLANES + L, :]
    for j in range(CONV_W - 1):
        off = SUBLANES - (CONV_W - 1) + j
        y = y + cw_ref[j:j + 1, :] * xbuf[off:off + L, :]
    xbuf[0:SUBLANES, :] = xbuf[L:L + SUBLANES, :]
    act = y * _sigmoid(y)

    gt = gt_ref[0] + bias_ref[...]
    lf_all = _log_sigmoid(gt) * LOG2E
    s_i = lax.broadcasted_iota(jnp.int32, (L, L), 0)
    t_i = lax.broadcasted_iota(jnp.int32, (L, L), 1)
    causal_t = s_i <= t_i
    tup = jnp.where(causal_t, 1.0, 0.0).astype(BF16)
    r_hi, r_mid, r_lo = _split3(lf_all)
    b = (_dot(r_hi, tup) + _dot(r_mid, tup) + _dot(r_lo, tup))[H:2 * H]
    i2 = gt[0:H] * LOG2E
    rowv = i2 - b
    lane = lax.broadcasted_iota(jnp.int32, (H, L), 1)
    cmax = rowv
    shift = 1
    while shift < L:
        cmax = jnp.maximum(cmax, jnp.where(lane >= shift, pltpu.roll(cmax, shift, 1), NEG))
        shift *= 2
    m_prev = m_scr[...]
    u = jnp.maximum(m_prev, cmax)
    w_inter = jnp.exp2(m_prev - u)
    inv_floor = jnp.exp2(-(b + u))
    g_tot = jnp.broadcast_to(b[:, L - 1:L], (H, L))
    a_w = g_tot - b + i2
    m_new = jnp.maximum(g_tot + m_prev, jnp.max(a_w, axis=1, keepdims=True))
    wk = jnp.exp2(a_w - m_new)
    decay = jnp.exp2(g_tot + m_prev - m_new)
    m_scr[...] = m_new

    rv_parts = [p.astype(F32) for p in _split3(rowv)]
    nu_parts = [p.astype(F32) for p in _split3(-u)]
    sub16 = lax.broadcasted_iota(jnp.int32, (2 * SUBLANES, L), 0)
    row16 = lambda x, h_: jnp.broadcast_to(x[h_:h_ + 1, :], (2 * SUBLANES, L))

    lane128 = lax.broadcasted_iota(jnp.int32, (L, LANES), 1)
    first_half = lane128 < ML_DQK
    ones_rows = jnp.ones((ML_ONES_ROWS, L), BF16)

    for h0 in range(0, H, ML_HEAD_GROUP):
        group = range(h0, h0 + ML_HEAD_GROUP)
        qbs, kbs, zs, s_ts, vas, nums = {}, {}, {}, {}, {}, {}
        for h in group:
            a_h = act[:, h * LANES:(h + 1) * LANES]
            qbs[h] = (jnp.where(first_half, a_h, 0.0) * (ML_DQK ** -0.5)).astype(BF16)
            kbs[h] = jnp.where(first_half, pltpu.roll(a_h, ML_DQK, 1), 0.0).astype(BF16)
        for h in group:
            src = jnp.where(sub16 == 0, row16(rv_parts[0], h), jnp.where(
                sub16 == 1, row16(rv_parts[1], h), jnp.where(
                    sub16 == 2, row16(rv_parts[2], h), jnp.where(sub16 < 6, 1.0, 0.0))))
            dst = jnp.where(sub16 < 3, 1.0, jnp.where(
                sub16 == 3, row16(nu_parts[0], h), jnp.where(
                    sub16 == 4, row16(nu_parts[1], h), jnp.where(
                        sub16 == 5, row16(nu_parts[2], h), 0.0))))
            zs[h] = _dot_tn(src.astype(BF16), dst.astype(BF16))
        for h in group:
            w_intra = jnp.exp2(jnp.where(causal_t, zs[h], NEG))
            s_ts[h] = (_dot_nt(kbs[h], qbs[h]) * w_intra).astype(BF16)
        for h in group:
            vas[h] = jnp.concatenate(
                [vo_ref[0, h * ML_DV:(h + 1) * ML_DV, :], ones_rows], axis=0)
            nums[h] = (_dot(vas[h], s_ts[h])
                       + _dot_nt(c_scr[h].astype(BF16), qbs[h]) * w_inter[h:h + 1, :])
        for h in group:
            num = nums[h]
            den = num[ML_DV:ML_DV + 1, :]
            h_t = num[:ML_DV, :] / jnp.maximum(jnp.abs(den), inv_floor[h:h + 1, :])
            hn = h_t * lax.rsqrt(jnp.mean(h_t * h_t, axis=0, keepdims=True) + EPS)
            hn = hn * hg_ref[h * ML_DV:(h + 1) * ML_DV, :]
            og = vo_ref[0, d + h * ML_DV:d + (h + 1) * ML_DV, :].astype(F32)
            out_ref[0, h * ML_DV:(h + 1) * ML_DV, :] = (_sigmoid(og) * hn).astype(out_ref.dtype)
        for h in group:
            vw = (vas[h].astype(F32) * wk[h:h + 1, :]).astype(BF16)
            c_scr[h] = decay[h:h + 1, 0:LANES] * c_scr[h] + _dot(vw, kbs[h])


def _mlstm(a_pre, vo_t, gates_t, conv_w, bias, hnorm_g, batch):
    m, d = a_pre.shape
    seq = m // batch
    L = ML_CHUNK
    nc = seq // L
    sub = PROJ_TM // L
    const = lambda b, c: (0, 0)
    tmin = lambda b, c: ((b * nc + c) // sub, 0, (b * nc + c) % sub)
    hg = jnp.broadcast_to(hnorm_g.reshape(d, 1), (d, L))
    bias_b = jnp.broadcast_to(bias.reshape(2 * ML_HEADS, 1), (2 * ML_HEADS, L))
    return pl.pallas_call(
        _mlstm_kernel,
        grid=(batch, nc),
        in_specs=[
            pl.BlockSpec((L, d), lambda b, c: (b * nc + c, 0)),
            pl.BlockSpec((1, 2 * d, L), tmin),
            pl.BlockSpec((1, 2 * ML_HEADS, L), tmin),
            _resident(conv_w.shape, const),
            _resident(bias_b.shape, const),
            _resident(hg.shape, const),
        ],
        out_specs=pl.BlockSpec((1, d, L), tmin),
        out_shape=jax.ShapeDtypeStruct((m // PROJ_TM, d, PROJ_TM), BF16),
        scratch_shapes=[
            pltpu.VMEM((SUBLANES + L, d), F32),
            pltpu.VMEM((ML_HEADS, ML_DV + ML_ONES_ROWS, LANES), F32),
            pltpu.VMEM((ML_HEADS, L), F32),
        ],
        compiler_params=_cparams(("arbitrary", "arbitrary")),
        name="mlstm_scan",
    )(a_pre, vo_t, gates_t, conv_w, bias_b, hg)


def _fox_prep_kernel(a_ref, f_ref, bf_ref, gqk_ref, place_ref, q_ref, k_ref, ends_ref, carry):
    ts = FX_TS
    H = FX_HEADS
    t = pl.program_id(1)

    @pl.when(t == 0)
    def _():
        carry[...] = jnp.zeros_like(carry)

    logf = _log_sigmoid(f_ref[...] + bf_ref[...]) * LOG2E
    r_i = lax.broadcasted_iota(jnp.int32, (ts, ts), 0)
    c_i = lax.broadcasted_iota(jnp.int32, (ts, ts), 1)
    tlow = jnp.where(r_i >= c_i, 1.0, 0.0).astype(BF16)
    hi, mid, lo = _split3(logf)
    fcum = _dot(tlow, hi) + _dot(tlow, mid) + _dot(tlow, lo) + carry[0:1, :]
    carry[...] = jnp.broadcast_to(fcum[ts - 1:ts, :], carry.shape)

    sub = lax.broadcasted_iota(jnp.int32, (SUBLANES, LANES), 0)
    ends_ref[0] = jnp.where(
        sub == 0, jnp.broadcast_to(fcum[0:1, :], (SUBLANES, LANES)),
        jnp.broadcast_to(fcum[ts - 1:ts, :], (SUBLANES, LANES)))

    lane = lax.broadcasted_iota(jnp.int32, (ts, LANES), 1)
    f_m = jnp.where(lane < H, fcum, 0.0)
    f_hi = f_m.astype(BF16).astype(F32)
    f_r1 = f_m - f_hi
    f_mid = f_r1.astype(BF16).astype(F32)
    f_lo = f_r1 - f_mid
    f_parts = (f_hi + pltpu.roll(f_mid, H, 1) + pltpu.roll(f_lo, 2 * H, 1)
               + jnp.where(lane == 3 * H, 1.0, 0.0)).astype(BF16)

    first_half = lane < FX_DH
    gqk = gqk_ref[...]
    for h in range(H):
        a = a_ref[:, h * LANES:(h + 1) * LANES].astype(F32)
        lhs = jnp.concatenate([(a * a).astype(BF16), f_parts], axis=1)
        res = _dot(lhs, place_ref[h])
        rinv = lax.rsqrt(res[:, :LANES] * (1.0 / FX_DH) + EPS)
        bias = res[:, LANES:]
        a_n = a * rinv * gqk
        q_ref[:, h * LANES:(h + 1) * LANES] = jnp.where(first_half, a_n, bias).astype(BF16)
        k_ref[:, h * LANES:(h + 1) * LANES] = pltpu.roll(
            jnp.where(first_half, bias, a_n), FX_DH, 1).astype(BF16)


def _fox_placement():
    H = FX_HEADS
    r = lax.broadcasted_iota(jnp.int32, (H, 2 * LANES, 2 * LANES), 1)
    c = lax.broadcasted_iota(jnp.int32, (H, 2 * LANES, 2 * LANES), 2)
    h = lax.broadcasted_iota(jnp.int32, (H, 2 * LANES, 2 * LANES), 0)
    half = (r < LANES) & (c < LANES) & ((r < FX_DH) == (c < FX_DH))
    src = r - LANES
    dst = c - LANES
    qb = FX_BIAS_LANE
    one_src = src == 3 * H
    plus = ((src == h) & (dst == qb)) | ((src == H + h) & (dst == qb + 1)) | (
        (src == 2 * H + h) & (dst == qb + 2)) | (one_src & (dst >= qb + 3) & (dst < qb + 6)) | (
        one_src & (dst >= 0) & (dst < 3))
    minus = ((src == h) & (dst == 3)) | ((src == H + h) & (dst == 4)) | (
        (src == 2 * H + h) & (dst == 5))
    in_place = (r >= LANES) & (c >= LANES)
    val = jnp.where(half | (in_place & plus), 1.0, jnp.where(in_place & minus, -1.0, 0.0))
    return val.astype(BF16)


def _fox_prep(a, fgate, bf_row, gqk, batch):
    m = a.shape[0]
    seq = m // batch
    ts = FX_TS
    nt = seq // ts
    wide = FX_HEADS * LANES
    row = lambda b, t: (b * nt + t, 0)
    const = lambda b, t: (0, 0)
    return pl.pallas_call(
        _fox_prep_kernel,
        grid=(batch, nt),
        in_specs=[
            pl.BlockSpec((ts, wide), row),
            pl.BlockSpec((ts, LANES), row),
            pl.BlockSpec((1, LANES), const),
            pl.BlockSpec((1, LANES), const),
            _resident((FX_HEADS, 2 * LANES, 2 * LANES), lambda b, t: (0, 0, 0)),
        ],
        out_specs=[pl.BlockSpec((ts, wide), row)] * 2
        + [pl.BlockSpec((1, SUBLANES, LANES), lambda b, t: (b * nt + t, 0, 0))],
        out_shape=[jax.ShapeDtypeStruct((m, wide), BF16)] * 2
        + [jax.ShapeDtypeStruct((m // ts, SUBLANES, LANES), F32)],
        scratch_shapes=[pltpu.VMEM((SUBLANES, LANES), F32)],
        compiler_params=_cparams(("arbitrary", "arbitrary")),
        name="fox_prep",
    )(a, fgate, bf_row, gqk, _fox_placement())


def _fox_attn_kernel(qtab_ref, ktab_ref, nit_ref, hperm_ref, qa_ref, qb_ref, ka_ref, kb_ref,
                     va_ref, vb_ref, oga_ref, ogb_ref, o_ref, m_st, acc_st, *score_scr,
                     n_diag, stride):
    tq, tk = FX_TQ, FX_TK
    del hperm_ref
    q_refs, k_refs = (qa_ref, qb_ref), (ka_ref, kb_ref)
    v_refs, og_refs = (va_ref, vb_ref), (oga_ref, ogb_ref)
    step = pl.program_id(0) * pl.num_programs(1) + pl.program_id(1)
    base = step * stride
    bufs = [(score_scr[2 * j], score_scr[2 * j + 1]) for j in range(FX_BUFS)]
    key_i = lax.broadcasted_iota(jnp.int32, (tk, tq), 0)
    qry_i = lax.broadcasted_iota(jnp.int32, (tk, tq), 1)
    causal = key_i <= qry_i
    ones_rows = jnp.ones((FX_ONES_ROWS, tk), BF16)

    def score(e, s_ref, mx_ref):
        q0 = pl.multiple_of(qtab_ref[base + e] * tq, tq)
        k0 = pl.multiple_of(ktab_ref[base + e] * tk, tk)
        for hh in range(2):
            q = q_refs[hh][0, pl.ds(q0, tq), :]
            k = k_refs[hh][0, pl.ds(k0, tk), :]
            s = _dot_nt(k, q)
            s_ref[hh] = s
            mx_ref[hh] = jnp.max(s, axis=0, keepdims=True)

    def values(kj, hh):
        return jnp.concatenate([v_refs[hh][0, kj], ones_rows], axis=0)

    def consume_diag(e, s_ref):
        qi = qtab_ref[base + e]
        for hh in range(2):
            s = jnp.where(causal, s_ref[hh], NEG)
            m_new = jnp.max(s, axis=0, keepdims=True)
            p = jnp.exp2(s - m_new).astype(BF16)
            acc_st[qi, hh] = _dot(values(qi, hh), p)
            m_st[qi, hh] = m_new

    def consume(e, s_ref, mx_ref):
        qi = qtab_ref[base + e]
        kj = ktab_ref[base + e]
        for hh in range(2):
            m_prev = m_st[qi, hh]
            m_new = jnp.maximum(m_prev, mx_ref[hh])
            p = jnp.exp2(s_ref[hh] - m_new).astype(BF16)
            corr = jnp.exp2(m_prev - m_new)
            acc_st[qi, hh] = corr * acc_st[qi, hh] + _dot(values(kj, hh), p)
            m_st[qi, hh] = m_new

    def diag_body(i, carry):
        for j in range(FX_UNROLL):
            e = FX_UNROLL * i + j
            score(e + FX_AHEAD, *bufs[(j + FX_AHEAD) % FX_BUFS])
            consume_diag(e, bufs[j % FX_BUFS][0])
        return carry

    def body(i, carry):
        for j in range(FX_UNROLL):
            e = FX_UNROLL * i + j
            score(e + FX_AHEAD, *bufs[(j + FX_AHEAD) % FX_BUFS])
            consume(e, *bufs[j % FX_BUFS])
        return carry

    for e in range(FX_AHEAD):
        score(e, *bufs[e])
    lax.fori_loop(0, n_diag // FX_UNROLL, diag_body, 0)
    lax.fori_loop(n_diag // FX_UNROLL, n_diag // FX_UNROLL + nit_ref[step], body, 0)

    def finalize(qi, carry):
        for hh in range(2):
            acc = acc_st[qi, hh]
            out = acc[:FX_DH, :] / acc[FX_DH:FX_DH + 1, :]
            og = og_refs[hh][0, qi].astype(F32)
            o_ref[0, qi, hh * FX_DH:(hh + 1) * FX_DH, :] = (_sigmoid(og) * out).astype(o_ref.dtype)
        return carry

    lax.fori_loop(0, n_diag, finalize, 0)


def _fox_block_tables(f_ends, qk_bound, hperm, batch, nq):
    pairs = FX_HEADS // 2
    f_first = f_ends[:, 0, :FX_HEADS].reshape(batch, nq, FX_HEADS)[:, :, hperm]
    f_last = f_ends[:, 1, :FX_HEADS].reshape(batch, nq, FX_HEADS)[:, :, hperm]
    f_first = f_first.reshape(batch, nq, pairs, 2)
    f_last = f_last.reshape(batch, nq, pairs, 2)
    qi = [i for i in range(nq) for _ in range(i)]
    kj = [j for i in range(nq) for j in range(i)]
    gap = f_last[:, jnp.asarray(kj)] - f_first[:, jnp.asarray(qi)]
    dead = jnp.all(gap > 2.0 * qk_bound + FX_PRUNE_GAP, axis=-1)
    dead = dead.transpose(0, 2, 1).astype(jnp.int32)
    full = lambda v: jnp.broadcast_to(jnp.asarray(v, jnp.int32), dead.shape)
    _, q_live_first, k_live_first = lax.sort(
        (dead, full(qi), full(kj)), dimension=-1, is_stable=True, num_keys=1)
    n_live = len(qi) - jnp.sum(dead, axis=-1)
    n_iter = (n_live + FX_UNROLL - 1) // FX_UNROLL
    diag = jnp.broadcast_to(jnp.arange(nq, dtype=jnp.int32), (batch, pairs, nq))
    pad = jnp.zeros((batch, pairs, FX_AHEAD), jnp.int32)
    qtab = jnp.concatenate([diag, q_live_first, pad], axis=-1)
    ktab = jnp.concatenate([diag, k_live_first, pad], axis=-1)
    return qtab.reshape(-1), ktab.reshape(-1), n_iter.reshape(-1), qtab.shape[-1]


def _fox_attn(q_aug, k_aug, vog_t, f_ends, qk_bound, hperm, batch):
    m, wide = q_aug.shape
    seq = m // batch
    tq, tk = FX_TQ, FX_TK
    assert tq == tk == PROJ_TM == FX_TS
    nq = seq // tq
    assert nq % FX_UNROLL == 0 and (nq * (nq - 1) // 2) % FX_UNROLL == 0
    pairs = FX_HEADS // 2
    H = FX_HEADS
    q3 = q_aug.reshape(batch, seq, wide)
    k3 = k_aug.reshape(batch, seq, wide)
    vog4 = vog_t.reshape(batch, nq, 2 * H * FX_DH, tk)
    qtab, ktab, n_iter, stride = _fox_block_tables(f_ends, qk_bound, hperm, batch, nq)
    rows = FX_DH + FX_ONES_ROWS

    def head_spec(shape, which, offset):
        if len(shape) == 3:
            return pl.BlockSpec(shape, lambda b, p, qt, kt, ni, hp: (b, 0, offset + hp[2 * p + which]))
        return pl.BlockSpec(shape, lambda b, p, qt, kt, ni, hp: (b, 0, offset + hp[2 * p + which], 0))

    grid_spec = pltpu.PrefetchScalarGridSpec(
        num_scalar_prefetch=4,
        grid=(batch, pairs),
        in_specs=[
            head_spec((1, seq, LANES), 0, 0), head_spec((1, seq, LANES), 1, 0),
            head_spec((1, seq, LANES), 0, 0), head_spec((1, seq, LANES), 1, 0),
            head_spec((1, nq, FX_DH, tk), 0, 0), head_spec((1, nq, FX_DH, tk), 1, 0),
            head_spec((1, nq, FX_DH, tk), 0, H), head_spec((1, nq, FX_DH, tk), 1, H),
        ],
        out_specs=pl.BlockSpec((1, nq, 2 * FX_DH, tq), lambda b, p, *_: (b, 0, p, 0)),
        scratch_shapes=[pltpu.VMEM((nq, 2, 1, tq), F32),
                        pltpu.VMEM((nq, 2, rows, tq), F32)]
        + [pltpu.VMEM((2, tk, tq), F32), pltpu.VMEM((2, 1, tq), F32)] * FX_BUFS,
    )
    out = pl.pallas_call(
        functools.partial(_fox_attn_kernel, n_diag=nq, stride=stride),
        grid_spec=grid_spec,
        out_shape=jax.ShapeDtypeStruct((batch, nq, H * FX_DH, tq), BF16),
        compiler_params=_cparams(("arbitrary", "arbitrary")),
        name="fox_attn",
    )(qtab, ktab, n_iter, hperm, q3, q3, k3, k3, vog4, vog4, vog4, vog4)
    return out.reshape(batch * nq, H * FX_DH, tq)


def _pair_heads(wq, wk, heads, dh):
    lead = wq.shape[0]
    return jnp.concatenate(
        [wq.reshape(lead, heads, dh), wk.reshape(lead, heads, dh)], axis=2
    ).reshape(lead, heads * 2 * dh)


def _pad_lanes(w):
    return jnp.zeros((w.shape[0], LANES), w.dtype).at[:, :w.shape[1]].set(w)


def kernel(x, c, ada_w, ada_b, norm_g, ffn_w_in, ffn_w_out, ml_w_in, ml_conv_w, ml_b_i, ml_b_f,
           ml_hnorm_g, ml_w_out, fx_w_in, fx_b_f, fx_qnorm_g, fx_knorm_g, fx_w_out, final_g):
    batch, seq, d = x.shape
    assert d == D_MODEL and seq % FFN_TM == 0 and seq % FX_TQ == 0 and seq % ML_CHUNK == 0
    depth = ada_w.shape[0]
    assert depth == 2 and batch <= SUBLANES

    mod = _modulation(c, ada_w, ada_b)
    w_in_b = ffn_w_in.astype(BF16)
    w_out_b = ffn_w_out.astype(BF16)
    x2 = x.reshape(batch * seq, d)

    x2 = _ffn(x2, mod, 0, 0, batch, norm_g[0, 0], w_in_b[0, 0], w_out_b[0, 0])

    H, dk, dv = ML_HEADS, ML_DQK, ML_DV
    w = ml_w_in[0]
    n_qk = H * dk
    w_a = _pair_heads(w[:, :n_qk], w[:, n_qk:2 * n_qk], H, dk)
    w_v = w[:, 2 * n_qk:2 * n_qk + H * dv]
    w_o = w[:, 2 * n_qk + H * dv:2 * n_qk + 2 * H * dv]
    w_g = w[:, 2 * n_qk + 2 * H * dv:]
    w_vo_t = jnp.concatenate([w_v, w_o], axis=1).T.astype(BF16)
    cw = ml_conv_w[0]
    conv_w = _pair_heads(cw[:, :n_qk], cw[:, n_qk:], H, dk)
    bias = jnp.concatenate([ml_b_i[0], ml_b_f[0]])
    a_pre, vo_t, gates_t = _proj(
        x2, mod, 0, batch, norm_g[0, 1], w_a.astype(BF16), (2 * n_qk,), (BF16,),
        (w_vo_t, w_g.T.astype(BF16)), (BF16, F32))
    hs_t = _mlstm(a_pre, vo_t, gates_t, conv_w, bias, ml_hnorm_g[0], batch)
    x2 = _ffn(x2, mod, 0, 2, batch, norm_g[0, 2], w_in_b[0, 1], w_out_b[0, 1],
              mixer_out=hs_t, w_mix=ml_w_out[0].astype(BF16))

    x2 = _ffn(x2, mod, 1, 0, batch, norm_g[1, 0], w_in_b[1, 0], w_out_b[1, 0])

    H, dh = FX_HEADS, FX_DH
    w = fx_w_in[0]
    n = H * dh
    w_a = _pair_heads(w[:, :n], w[:, n:2 * n], H, dh)
    w_main = jnp.concatenate([w_a, _pad_lanes(w[:, 4 * n:])], axis=1)
    a, fgate, vog_t = _proj(
        x2, mod, 1, batch, norm_g[1, 1], w_main.astype(BF16), (2 * n, LANES),
        (BF16, F32), (w[:, 2 * n:4 * n].T.astype(BF16),), (BF16,))
    gqk = jnp.concatenate(
        [fx_qnorm_g[0] * (dh ** -0.5 * LOG2E), fx_knorm_g[0]]).reshape(1, LANES)
    bf_row = _pad_lanes(fx_b_f[0].reshape(1, H))
    q_aug, k_aug, f_ends = _fox_prep(a, fgate, bf_row, gqk, batch)
    qk_bound = (FX_BOUND_SLACK * dh * dh ** -0.5 * LOG2E
                * jnp.max(jnp.abs(fx_qnorm_g[0])) * jnp.max(jnp.abs(fx_knorm_g[0])))
    hperm = jnp.argsort(fx_b_f[0]).astype(jnp.int32)
    attn_t = _fox_attn(q_aug, k_aug, vog_t, f_ends, qk_bound, hperm, batch)
    w_mix = fx_w_out[0].reshape(H, dh, d)[hperm].reshape(H * dh, d).astype(BF16)
    x2 = _ffn(x2, mod, 1, 2, batch, norm_g[1, 2], w_in_b[1, 1], w_out_b[1, 1], final_g=final_g,
              mixer_out=attn_t, w_mix=w_mix)
    return x2.reshape(batch, seq, d)
```
